```python
import jax, jax.numpy as jnp
from jax import lax
import numpy as np

D_MODEL = 1024
BATCH = 2
SEQ = 8192
DEPTH = 2
DEC_BATCH = 128
DEC_SEQ = 1
PAST_LEN = 2048
PAGE_SIZE = 128

MLSTM_HEADS = 4
MLSTM_DV = D_MODEL // 2 // MLSTM_HEADS
MLSTM_DK = MLSTM_DV // 2
MLSTM_CHUNK = 128
FOX_HEADS = 8
FOX_DH = D_MODEL // 2 // FOX_HEADS
Q_BLOCK = 128
RWKV_DH = 64
RWKV_HEADS = D_MODEL // RWKV_DH
RWKV_DECAY_RANK = 64
RWKV_A_RANK = 64
RWKV_G_RANK = 128
FFN_HIDDEN = -(-8 * D_MODEL // (3 * 256)) * 256
EPS = 1e-6
GN_EPS = 64e-5
IN_SIZES = (MLSTM_HEADS * MLSTM_DK, MLSTM_HEADS * MLSTM_DK, MLSTM_HEADS * MLSTM_DV, MLSTM_HEADS * MLSTM_DV,
            MLSTM_HEADS, MLSTM_HEADS, FOX_HEADS * FOX_DH, FOX_HEADS * FOX_DH, FOX_HEADS * FOX_DH, FOX_HEADS)
IN_COLS = sum(IN_SIZES)
MIX_WIDTH = MLSTM_HEADS * MLSTM_DV + FOX_HEADS * FOX_DH

kernel_name = 'mlstm_fox_rwkv7_hybrid_step'


def rmsnorm(x, g):
    x32 = x.astype(jnp.float32)
    y = x32 * lax.rsqrt(jnp.mean(x32 * x32, axis=-1, keepdims=True) + EPS)
    return (y * g.astype(jnp.float32)).astype(x.dtype)


def swiglu(h, w_gate, w_up, w_down):
    return (jax.nn.silu(h @ w_gate) * (h @ w_up)) @ w_down


def mlstm_chunkwise(q, k, v, ig, lf, c0, n0, m0):
    B, T = q.shape[0], q.shape[1]
    L = min(MLSTM_CHUNK, T)
    nc = -(-T // L)
    pad = nc * L - T
    f32 = jnp.float32

    def blocks(a, fill):
        a = a.astype(f32)
        a = jnp.pad(a, [(0, 0), (0, pad)] + [(0, 0)] * (a.ndim - 2), constant_values=fill)
        a = a.reshape((B, nc, L) + a.shape[2:])
        return jnp.swapaxes(jnp.moveaxis(a, 1, 0), 2, 3)

    xs = (blocks(q, 0.0), blocks(k, 0.0), blocks(v, 0.0), blocks(ig, -jnp.inf), blocks(lf, 0.0))
    causal = jnp.tril(jnp.ones((L, L), dtype=bool))

    def step(carry, inp):
        c, n, m = carry
        qc, kc, vc, igc, lfc = inp
        b = jnp.cumsum(lfc, axis=-1)
        d = jnp.where(causal, b[..., :, None] - b[..., None, :] + igc[..., None, :], -jnp.inf)
        inter = b + m[..., None]
        m_t = jnp.maximum(inter, jnp.max(d, axis=-1))
        w_intra = jnp.exp(d - m_t[..., None])
        w_inter = jnp.exp(inter - m_t)
        s = jnp.einsum('bhtd,bhsd->bhts', qc, kc) * w_intra
        num = w_inter[..., None] * jnp.einsum('bhtd,bhde->bhte', qc, c) + jnp.einsum('bhts,bhse->bhte', s, vc)
        den = w_inter * jnp.einsum('bhtd,bhd->bht', qc, n) + jnp.sum(s, axis=-1)
        h = num / jnp.maximum(jnp.abs(den), jnp.exp(-m_t))[..., None]
        b_last = b[..., -1]
        g = b_last[..., None] - b + igc
        m_new = jnp.maximum(b_last + m, jnp.max(g, axis=-1))
        w_k = jnp.exp(g - m_new[..., None])
        decay = jnp.exp(b_last + m - m_new)
        c_new = decay[..., None, None] * c + jnp.einsum('bhs,bhsd,bhse->bhde', w_k, kc, vc)
        n_new = decay[..., None] * n + jnp.einsum('bhs,bhsd->bhd', w_k, kc)
        return (c_new, n_new, m_new), h

    (c, n, m), hs = lax.scan(step, (c0.astype(f32), n0.astype(f32), m0.astype(f32)), xs)
    hs = jnp.moveaxis(jnp.swapaxes(hs, 2, 3), 0, 1)
    hs = hs.reshape((B, nc * L) + hs.shape[3:])[:, :T]
    return hs, c, n, m


def fox_attention(q, k, v, fq, fk, q_pos):
    B, Tq, H, Dh = q.shape
    Tk = k.shape[1]
    qb = min(Q_BLOCK, Tq)
    nb = -(-Tq // qb)
    pad = nb * qb - Tq
    q = jnp.pad(q, ((0, 0), (0, pad), (0, 0), (0, 0)))
    fq = jnp.pad(fq, ((0, 0), (0, pad), (0, 0)))
    q_pos = jnp.pad(q_pos, (0, pad))
    qs = jnp.moveaxis(q.reshape(B, nb, qb, H, Dh), 1, 0)
    fqs = jnp.moveaxis(fq.reshape(B, nb, qb, H), 1, 0)
    ps = q_pos.reshape(nb, qb)
    fk_t = jnp.swapaxes(fk, 1, 2)
    key_pos = jnp.arange(Tk, dtype=jnp.int32)
    scale = Dh ** -0.5

    def block(args):
        qblk, fqblk, pblk = args
        s = jnp.einsum('bqhd,bkhd->bhqk', qblk, k).astype(jnp.float32) * scale
        s = s + jnp.swapaxes(fqblk, 1, 2)[..., None] - fk_t[:, :, None, :]
        s = jnp.where(key_pos[None, :] <= pblk[:, None], s, -jnp.inf)
        p = jax.nn.softmax(s, axis=-1)
        return jnp.einsum('bhqk,bkhd->bqhd', p.astype(v.dtype), v)

    out = lax.map(block, (qs, fqs, ps))
    return jnp.moveaxis(out, 0, 1).reshape(B, nb * qb, H, Dh)[:, :Tq]


def hybrid_mixer(h, w_in, i_bias, f_bias, out_g, fox_bias, w_out, c0, n0, m0, fox_past):
    B, T, _ = h.shape
    f32 = jnp.float32
    split_at = [int(i) for i in np.cumsum(IN_SIZES)[:-1]]
    mq, mk, mv, mo, mi, mf, fq, fk, fv, ff = jnp.split(h @ w_in, split_at, axis=-1)
    q = mq.reshape(B, T, MLSTM_HEADS, MLSTM_DK)
    k = mk.reshape(B, T, MLSTM_HEADS, MLSTM_DK) * (MLSTM_DK ** -0.5)
    v = mv.reshape(B, T, MLSTM_HEADS, MLSTM_DV)
    ig = mi.astype(f32) + i_bias.astype(f32)
    lf = jax.nn.log_sigmoid(mf.astype(f32) + f_bias.astype(f32))
    hm, c, n, m = mlstm_chunkwise(q, k, v, ig, lf, c0, n0, m0)
    hm = hm * lax.rsqrt(jnp.mean(hm * hm, axis=-1, keepdims=True) + EPS)
    hm = hm.reshape(B, T, MLSTM_HEADS * MLSTM_DV) * out_g.astype(f32) * jax.nn.sigmoid(mo.astype(f32))
    qf = fq.reshape(B, T, FOX_HEADS, FOX_DH)
    kf = fk.reshape(B, T, FOX_HEADS, FOX_DH)
    vf = fv.reshape(B, T, FOX_HEADS, FOX_DH)
    lff = jax.nn.log_sigmoid(ff.astype(f32) + fox_bias.astype(f32))
    if fox_past is None:
        k_all, v_all, lf_all, p0 = kf, vf, lff, 0
    else:
        kp, vp, lfp = fox_past
        k_all = jnp.concatenate([kp.astype(kf.dtype), kf], axis=1)
        v_all = jnp.concatenate([vp.astype(vf.dtype), vf], axis=1)
        lf_all = jnp.concatenate([lfp.astype(f32), lff], axis=1)
        p0 = kp.shape[1]
    cum = jnp.cumsum(lf_all, axis=1)
    hf = fox_attention(qf, k_all, v_all, cum[:, p0:], cum, p0 + jnp.arange(T, dtype=jnp.int32))
    mixed = jnp.concatenate([hm.astype(h.dtype), hf.reshape(B, T, FOX_HEADS * FOX_DH).astype(h.dtype)], axis=-1)
    dt = h.dtype
    return mixed @ w_out, (kf, vf, lff.astype(dt), c.astype(dt), n.astype(dt), m.astype(dt))


def wkv7_scan(r, w, k, v, a, b, s0):
    def step(S, inp):
        rt, wt, kt, vt, at, bt = inp
        S = (S * wt[:, :, None, :] + jnp.einsum('bhvk,bhk->bhv', S, at)[..., None] * bt[:, :, None, :]
             + vt[..., None] * kt[:, :, None, :])
        return S, jnp.einsum('bhvk,bhk->bhv', S, rt)

    xs = tuple(jnp.moveaxis(t.astype(jnp.float32), 1, 0) for t in (r, w, k, v, a, b))
    S, ys = lax.scan(step, s0.astype(jnp.float32), xs)
    return jnp.moveaxis(ys, 0, 1), S


def rwkv7_mixer(h, mix, w_r, w_k, w_v, w_o, w0, w1, w2, a0, a1, a2, g1, g2, k_k, k_a, r_k, ln_w, ln_b, s0, shift0):
    B, T, D = h.shape
    f32 = jnp.float32
    x_prev = jnp.concatenate([shift0[:, None, :].astype(h.dtype), h[:, :-1]], axis=1)
    xx = x_prev - h
    xr, xw, xk, xv, xa, xg = [h + xx * mix[i] for i in range(6)]
    r = xr @ w_r
    k = xk @ w_k
    v = xv @ w_v
    w_log = -jax.nn.softplus(-(w0 + jnp.tanh(xw @ w1) @ w2).astype(f32)) - 0.5
    decay = jnp.exp(-jnp.exp(w_log))
    a = jax.nn.sigmoid((a0 + (xa @ a1) @ a2).astype(f32))
    g = jax.nn.sigmoid(xg @ g1) @ g2
    heads = lambda t: t.reshape(B, T, RWKV_HEADS, RWKV_DH)
    kk = heads((k * k_k).astype(f32))
    kk = kk / jnp.maximum(jnp.sqrt(jnp.sum(kk * kk, axis=-1, keepdims=True)), 1e-12)
    k = k.astype(f32) * (1.0 + (a - 1.0) * k_a.astype(f32))
    rh, kh, vh, ah = heads(r.astype(f32)), heads(k), heads(v.astype(f32)), heads(a)
    y, s_new = wkv7_scan(rh, heads(decay), kh, vh, -kk, kk * ah, s0)
    mu = jnp.mean(y, axis=-1, keepdims=True)
    var = jnp.mean(jnp.square(y - mu), axis=-1, keepdims=True)
    y = ((y - mu) * lax.rsqrt(var + GN_EPS)).reshape(B, T, D) * ln_w.astype(f32) + ln_b.astype(f32)
    bonus = jnp.sum(rh * kh * r_k.astype(f32), axis=-1, keepdims=True) * vh
    y = (y + bonus.reshape(B, T, D)) * g.astype(f32)
    return y.astype(h.dtype) @ w_o, (s_new.astype(h.dtype), h[:, -1])


def setup_inputs(seed: int = 0) -> dict:
    key = jax.random.key(seed)
    ks = iter(jax.random.split(key, 64))
    nrm = lambda shape, scale: scale * jax.random.normal(next(ks), shape, jnp.float32)
    uni = lambda shape, lo, hi: jax.random.uniform(next(ks), shape, jnp.float32, lo, hi)
    n_even, n_odd = (DEPTH + 1) // 2, DEPTH // 2
    n_pages = PAST_LEN // PAGE_SIZE
    n_used = DEC_BATCH * n_pages
    n_phys = n_used + max(1, n_used // 4)
    D = D_MODEL
    page_table = jax.random.permutation(next(ks), n_phys)[:n_used].reshape(DEC_BATCH, n_pages).astype(jnp.int32)
    return {
        'x_prompt': nrm((BATCH, SEQ, D), 1.0),
        'x_sample': nrm((DEC_BATCH, DEC_SEQ, D), 1.0),
        'cache_fox_k': nrm((n_even, n_phys, PAGE_SIZE, FOX_HEADS, FOX_DH), 1.0),
        'cache_fox_v': nrm((n_even, n_phys, PAGE_SIZE, FOX_HEADS, FOX_DH), 1.0),
        'cache_fox_lf': jax.nn.log_sigmoid(3.0 + nrm((n_even, n_phys, PAGE_SIZE, FOX_HEADS), 1.0)),
        'state_mlstm_c': nrm((n_even, DEC_BATCH, MLSTM_HEADS, MLSTM_DK, MLSTM_DV), 0.5),
        'state_mlstm_n': nrm((n_even, DEC_BATCH, MLSTM_HEADS, MLSTM_DK), 0.5),
        'state_mlstm_m': nrm((n_even, DEC_BATCH, MLSTM_HEADS), 1.0),
        'state_rwkv_s': nrm((n_odd, DEC_BATCH, RWKV_HEADS, RWKV_DH, RWKV_DH), 0.3),
        'state_rwkv_shift': nrm((n_odd, DEC_BATCH, D), 1.0),
        'page_table': page_table,
        'norm_g': 1.0 + nrm((DEPTH, 4, D), 0.05),
        'hyb_w_in': nrm((n_even, D, IN_COLS), D ** -0.5),
        'mlstm_i_bias': nrm((n_even, MLSTM_HEADS), 0.1),
        'mlstm_f_bias': uni((n_even, MLSTM_HEADS), 3.0, 6.0),
        'mlstm_out_g': 1.0 + nrm((n_even, MLSTM_HEADS * MLSTM_DV), 0.05),
        'fox_f_bias': uni((n_even, FOX_HEADS), 2.0, 5.0),
        'hyb_w_out': nrm((n_even, MIX_WIDTH, D), MIX_WIDTH ** -0.5),
        'rwkv_mix': uni((n_odd, 6, D), 0.0, 1.0),
        'rwkv_w_r': nrm((n_odd, D, D), D ** -0.5),
        'rwkv_w_k': nrm((n_odd, D, D), D ** -0.5),
        'rwkv_w_v': nrm((n_odd, D, D), D ** -0.5),
        'rwkv_w_o': nrm((n_odd, D, D), D ** -0.5),
        'rwkv_w0': -1.5 + nrm((n_odd, D), 0.5),
        'rwkv_w1': nrm((n_odd, D, RWKV_DECAY_RANK), D ** -0.5),
        'rwkv_w2': nrm((n_odd, RWKV_DECAY_RANK, D), 0.1 * RWKV_DECAY_RANK ** -0.5),
        'rwkv_a0': nrm((n_odd, D), 0.1),
        'rwkv_a1': nrm((n_odd, D, RWKV_A_RANK), D ** -0.5),
        'rwkv_a2': nrm((n_odd, RWKV_A_RANK, D), 0.5 * RWKV_A_RANK ** -0.5),
        'rwkv_g1': nrm((n_odd, D, RWKV_G_RANK), D ** -0.5),
        'rwkv_g2': nrm((n_odd, RWKV_G_RANK, D), RWKV_G_RANK ** -0.5),
        'rwkv_k_k': 0.85 + nrm((n_odd, D), 0.05),
        'rwkv_k_a': 1.0 + nrm((n_odd, D), 0.05),
        'rwkv_r_k': nrm((n_odd, RWKV_HEADS, RWKV_DH), 0.1),
        'rwkv_ln_w': 1.0 + nrm((n_odd, D), 0.05),
        'rwkv_ln_b': nrm((n_odd, D), 0.02),
        'ffn_w_gate': nrm((DEPTH, D, FFN_HIDDEN), D ** -0.5),
        'ffn_w_up': nrm((DEPTH, D, FFN_HIDDEN), D ** -0.5),
        'ffn_w_down': nrm((DEPTH, FFN_HIDDEN, D), FFN_HIDDEN ** -0.5),
    }


def reference(x_prompt, x_sample, cache_fox_k, cache_fox_v, cache_fox_lf, state_mlstm_c, state_mlstm_n,
              state_mlstm_m, state_rwkv_s, state_rwkv_shift, page_table, norm_g, hyb_w_in, mlstm_i_bias,
              mlstm_f_bias, mlstm_out_g, fox_f_bias, hyb_w_out, rwkv_mix, rwkv_w_r, rwkv_w_k, rwkv_w_v, rwkv_w_o,
              rwkv_w0, rwkv_w1, rwkv_w2, rwkv_a0, rwkv_a1, rwkv_a2, rwkv_g1, rwkv_g2, rwkv_k_k, rwkv_k_a, rwkv_r_k,
              rwkv_ln_w, rwkv_ln_b, ffn_w_gate, ffn_w_up, ffn_w_down):
    n_even, n_odd = (DEPTH + 1) // 2, DEPTH // 2

    def trunk(x, even_init, odd_init, fox_pasts):
        even_new, odd_new = [], []
        for l in range(DEPTH):
            h = rmsnorm(x, norm_g[l, 0])
            if l % 2 == 0:
                e = l // 2
                c0, n0, m0 = even_init[e]
                out, st = hybrid_mixer(h, hyb_w_in[e], mlstm_i_bias[e], mlstm_f_bias[e], mlstm_out_g[e],
                                       fox_f_bias[e], hyb_w_out[e], c0, n0, m0, fox_pasts[e])
                even_new.append(st)
            else:
                o = l // 2
                s0, sh0 = odd_init[o]
                out, st = rwkv7_mixer(h, rwkv_mix[o], rwkv_w_r[o], rwkv_w_k[o], rwkv_w_v[o], rwkv_w_o[o],
                                      rwkv_w0[o], rwkv_w1[o], rwkv_w2[o], rwkv_a0[o], rwkv_a1[o], rwkv_a2[o],
                                      rwkv_g1[o], rwkv_g2[o], rwkv_k_k[o], rwkv_k_a[o], rwkv_r_k[o],
                                      rwkv_ln_w[o], rwkv_ln_b[o], s0, sh0)
                odd_new.append(st)
            x = x + rmsnorm(out, norm_g[l, 1])
            h = rmsnorm(x, norm_g[l, 2])
            x = x + rmsnorm(swiglu(h, ffn_w_gate[l], ffn_w_up[l], ffn_w_down[l]), norm_g[l, 3])
        return x, even_new, odd_new

    bp, dt = x_prompt.shape[0], x_prompt.dtype
    p_even_init = [(jnp.zeros((bp, MLSTM_HEADS, MLSTM_DK, MLSTM_DV), dt), jnp.zeros((bp, MLSTM_HEADS, MLSTM_DK), dt),
                    jnp.zeros((bp, MLSTM_HEADS), dt)) for _ in range(n_even)]
    p_odd_init = [(jnp.zeros((bp, RWKV_HEADS, RWKV_DH, RWKV_DH), dt), jnp.zeros((bp, D_MODEL), dt))
                  for _ in range(n_odd)]
    y_prompt, pe, po = trunk(x_prompt, p_even_init, p_odd_init, [None] * n_even)

    def gather(pool):
        g = pool[page_table]
        return g.reshape((g.shape[0], g.shape[1] * g.shape[2]) + g.shape[3:])

    fox_pasts = [(gather(cache_fox_k[e]), gather(cache_fox_v[e]), gather(cache_fox_lf[e])) for e in range(n_even)]
    s_even_init = [(state_mlstm_c[e], state_mlstm_n[e], state_mlstm_m[e]) for e in range(n_even)]
    s_odd_init = [(state_rwkv_s[o], state_rwkv_shift[o]) for o in range(n_odd)]
    y_sample, se, so = trunk(x_sample, s_even_init, s_odd_init, fox_pasts)

    stk = lambda lst, i: jnp.stack([s[i] for s in lst])
    return (y_prompt, y_sample,
            stk(pe, 0), stk(pe, 1), stk(pe, 2), stk(pe, 3), stk(pe, 4), stk(pe, 5), stk(po, 0), stk(po, 1),
            stk(se, 0), stk(se, 1), stk(se, 2), stk(se, 3), stk(se, 4), stk(se, 5), stk(so, 0), stk(so, 1))
```

```python
import functools

import jax
import jax.numpy as jnp
from jax import lax
from jax.experimental import pallas as pl
from jax.experimental.pallas import tpu as pltpu

F32, BF16 = jnp.float32, jnp.bfloat16
EPS = 1e-6
GN_EPS = 64e-5
NEG = -1e30

MLSTM_HEADS, MLSTM_DK, MLSTM_DV = 4, 64, 128
FOX_HEADS, FOX_DH = 8, 64
RWKV_HEADS, RWKV_DH = 16, 64
MLSTM_CHUNK = 128
RWKV_CHUNK = 64
LANES = 128
ATTN_BLOCK = 512
ROW_TILE = 512
VMEM_LIMIT = 56 * 1024 * 1024


def _cparams(sem):
    return pltpu.CompilerParams(dimension_semantics=sem, vmem_limit_bytes=VMEM_LIMIT)


def _const_spec(shape):
    nd = len(shape)
    return pl.BlockSpec(shape, lambda *a: (0,) * nd, pipeline_mode=pl.Buffered(1))


def _dot(a, b):
    return jnp.dot(a.astype(BF16), b.astype(BF16), preferred_element_type=F32)


def _dot_nt(a, b):
    return lax.dot_general(a.astype(BF16), b.astype(BF16), (((1,), (1,)), ((), ())),
                           preferred_element_type=F32)


def _dot_tn(a, b):
    return lax.dot_general(a.astype(BF16), b.astype(BF16), (((0,), (0,)), ((), ())),
                           preferred_element_type=F32)


def _split3(x):
    hi = x.astype(BF16)
    r1 = x - hi.astype(F32)
    mid = r1.astype(BF16)
    lo = (r1 - mid.astype(F32)).astype(BF16)
    return hi, mid, lo


def _dot3_left(m, x):
    hi, mid, lo = _split3(x)
    d = lambda t: jnp.dot(m, t, preferred_element_type=F32)
    return (d(lo) + d(mid)) + d(hi)


def _dot3_right(x, m):
    hi, mid, lo = _split3(x)
    d = lambda t: jnp.dot(t, m, preferred_element_type=F32)
    return (d(lo) + d(mid)) + d(hi)


def _rms(x, g):
    return x * lax.rsqrt(jnp.mean(x * x, axis=-1, keepdims=True) + EPS) * g


def _sigmoid(x):
    return 1.0 / (1.0 + jnp.exp(-x))


def _log_sigmoid(x):
    return jnp.minimum(x, 0.0) - jnp.log1p(jnp.exp(-jnp.abs(x)))


def _softplus(x):
    return jnp.maximum(x, 0.0) + jnp.log1p(jnp.exp(-jnp.abs(x)))


def _iota(shape, dim):
    return lax.broadcasted_iota(jnp.int32, shape, dim)


def _tri(n, kind):
    r, c = _iota((n, n), 0), _iota((n, n), 1)
    m = {"lower": r >= c, "upper": r <= c, "strict_upper_t": r > c}[kind]
    return jnp.where(m, 1.0, 0.0).astype(BF16)


def _hyb_in_kernel(x_ref, g_ref, w_ref, wg_ref, bias_ref,
                   mq_ref, mk_ref, mv_ref, mo_ref, fq_ref, fk_ref, fv_ref, fk16_ref, fv16_ref,
                   gcol_ref, *maybe_grow):
    hn = _rms(x_ref[...], g_ref[...]).astype(BF16)
    seg = lambda lo, hi: jnp.dot(hn, w_ref[:, lo:hi], preferred_element_type=F32)
    mq_ref[...] = seg(0, 256).astype(BF16)
    mk_ref[...] = (seg(256, 512) * (MLSTM_DK ** -0.5)).astype(BF16)
    mv_ref[...] = seg(512, 1024).astype(BF16)
    mo_ref[...] = seg(1024, 1536)
    fq_ref[...] = (seg(1536, 2048) * (FOX_DH ** -0.5)).astype(BF16)
    fk = seg(2048, 2560)
    fk_ref[...] = fk
    fk16_ref[...] = fk.astype(BF16)
    fv = seg(2560, 3072)
    fv_ref[...] = fv
    fv16_ref[...] = fv.astype(BF16)
    gates = jnp.dot(hn, wg_ref[...], preferred_element_type=F32) + bias_ref[...]
    lane = _iota(gates.shape, 1)
    gc = jnp.where(lane < MLSTM_HEADS, gates, _log_sigmoid(gates))
    gcol_ref[...] = gc
    if maybe_grow:
        maybe_grow[0][...] = gc.T


def _hyb_in(x, g, w, wg, bias, tm, with_rows):
    n, d = x.shape
    row = lambda c: pl.BlockSpec((tm, c), lambda i: (i, 0))
    outs = [(256, BF16), (256, BF16), (512, BF16), (512, F32), (512, BF16), (512, F32), (512, F32),
            (512, BF16), (512, BF16), (LANES, F32)]
    out_shape = [jax.ShapeDtypeStruct((n, c), t) for c, t in outs]
    out_specs = [row(c) for c, _ in outs]
    if with_rows:
        out_shape.append(jax.ShapeDtypeStruct((LANES, n), F32))
        out_specs.append(pl.BlockSpec((LANES, tm), lambda i: (0, i)))
    return pl.pallas_call(
        _hyb_in_kernel, grid=(n // tm,),
        in_specs=[row(d), _const_spec((1, d)), _const_spec(w.shape), _const_spec(wg.shape),
                  _const_spec((1, LANES))],
        out_specs=out_specs, out_shape=out_shape,
        compiler_params=_cparams(("parallel",)),
    )(x, g, w, wg, bias)


def _fox_prep_kernel(fq_ref, fk_ref, gcol_ref, selq_ref, selk_ref, cq_ref, ck_ref,
                     qa_ref, ka_ref, carry_ref):
    @pl.when(pl.program_id(1) == 0)
    def _():
        carry_ref[...] = jnp.zeros_like(carry_ref)

    tm = gcol_ref.shape[0]
    cum = _dot3_left(_tri(tm, "lower"), gcol_ref[...]) + carry_ref[...]
    carry_ref[...] = cum[tm - 1:tm, :]
    hi, mid, lo = _split3(cum)
    xq = jnp.concatenate([fq_ref[...], hi, mid, lo], axis=1)
    xk = jnp.concatenate([fk_ref[...], hi, mid, lo], axis=1)
    qa_ref[...] = (jnp.dot(xq, selq_ref[...], preferred_element_type=F32) + cq_ref[...]).astype(BF16)
    ka_ref[...] = (jnp.dot(xk, selk_ref[...], preferred_element_type=F32) + ck_ref[...]).astype(BF16)


def _fox_aug_constants():
    h = jnp.arange(FOX_HEADS)
    d = jnp.arange(FOX_DH)
    rows_qk = (h[:, None] * FOX_DH + d[None, :]).reshape(-1)
    cols_qk = (h[:, None] * LANES + d[None, :]).reshape(-1)
    base = 4 * LANES
    selq = jnp.zeros((4 * LANES + 3 * LANES, FOX_HEADS * LANES), F32)
    selk = jnp.zeros_like(selq)
    selq = selq.at[rows_qk, cols_qk].set(1.0)
    selk = selk.at[rows_qk, cols_qk].set(1.0)
    cq = jnp.zeros((1, FOX_HEADS * LANES), F32)
    ck = jnp.zeros_like(cq)
    for part in range(3):
        src = base + part * LANES + 2 * MLSTM_HEADS + h
        selq = selq.at[src, h * LANES + FOX_DH + part].set(1.0)
        selk = selk.at[src, h * LANES + FOX_DH + 3 + part].set(-1.0)
        cq = cq.at[0, h * LANES + FOX_DH + 3 + part].set(1.0)
        ck = ck.at[0, h * LANES + FOX_DH + part].set(1.0)
    return selq.astype(BF16), selk.astype(BF16), cq, ck


def _fox_prep(fq, fk16, gcol, b, t, tm):
    n = b * t
    nt = t // tm
    selq, selk, cq, ck = _fox_aug_constants()
    row = lambda c: pl.BlockSpec((tm, c), lambda bi, i: (bi * nt + i, 0))
    width = FOX_HEADS * LANES
    return pl.pallas_call(
        _fox_prep_kernel, grid=(b, nt),
        in_specs=[row(512), row(512), row(LANES), _const_spec(selq.shape), _const_spec(selk.shape),
                  _const_spec(cq.shape), _const_spec(ck.shape)],
        out_specs=[row(width), row(width)],
        out_shape=[jax.ShapeDtypeStruct((n, width), BF16)] * 2,
        scratch_shapes=[pltpu.VMEM((1, LANES), F32)],
        compiler_params=_cparams(("parallel", "arbitrary")),
    )(fq, fk16, gcol, selq, selk, cq, ck)


def _flash_kernel(q_ref, k_ref, v_ref, o_ref, *, blk):
    i = pl.program_id(2)
    qs = (q_ref[:, 0:LANES], q_ref[:, LANES:2 * LANES])
    causal = _iota((blk, blk), 0) >= _iota((blk, blk), 1)

    def step(j, carry, masked):
        off = pl.multiple_of(j * blk, blk)
        kb = k_ref[pl.ds(off, blk), :]
        vb = v_ref[pl.ds(off, blk), :]
        out = []
        for hh in range(2):
            m, l, acc = carry[hh]
            s = _dot_nt(qs[hh], kb[:, hh * LANES:(hh + 1) * LANES])
            if masked:
                s = jnp.where(causal, s, NEG)
            mn = jnp.maximum(m, jnp.max(s, axis=1, keepdims=True))
            al = jnp.exp(m - mn)
            p = jnp.exp(s - mn)
            l = al * l + jnp.sum(p, axis=1, keepdims=True)
            acc = al * acc + jnp.dot(p.astype(BF16), vb, preferred_element_type=F32)
            out.append((mn, l, acc))
        return tuple(out)

    init = tuple((jnp.full((blk, 1), NEG, F32), jnp.zeros((blk, 1), F32), jnp.zeros((blk, LANES), F32))
                 for _ in range(2))
    carry = lax.fori_loop(0, i, lambda j, c: step(j, c, False), init)
    carry = step(i, carry, True)
    (_, l0, a0), (_, l1, a1) = carry
    lane = _iota((blk, LANES), 1)
    o_ref[...] = jnp.where(lane < FOX_DH, a0 / l0, a1 / l1).astype(BF16)


def _fox_flash(q_aug, k_aug, v16, b, t, blk):
    nq = t // blk
    pairs = FOX_HEADS // 2
    return pl.pallas_call(
        functools.partial(_flash_kernel, blk=blk), grid=(b, pairs, nq),
        in_specs=[pl.BlockSpec((blk, 2 * LANES), lambda bi, p, i: (bi * nq + i, p)),
                  pl.BlockSpec((t, 2 * LANES), lambda bi, p, i: (bi, p)),
                  pl.BlockSpec((t, LANES), lambda bi, p, i: (bi, p))],
        out_specs=pl.BlockSpec((blk, LANES), lambda bi, p, i: (bi * nq + i, p)),
        out_shape=jax.ShapeDtypeStruct((b * t, FOX_HEADS * FOX_DH), BF16),
        compiler_params=_cparams(("parallel", "parallel", "arbitrary")),
    )(q_aug, k_aug, v16)


def _mlstm_chunk_kernel(q_ref, k_ref, v_ref, o_ref, gcol_ref, grow_ref, outg_ref, c0_ref, n0_ref, m0_ref,
                        hm_ref, c_out, n_out, m_out, sc_ref, ms_ref):
    L = q_ref.shape[0]
    ci = pl.program_id(1)

    @pl.when(ci == 0)
    def _():
        sc_ref[:, :, 0:MLSTM_DV] = c0_ref[...]
        sc_ref[:, :, MLSTM_DV:] = n0_ref[...]
        ms_ref[...] = m0_ref[...]

    gcol = gcol_ref[...]
    grow = grow_ref[...]
    bcol_all = _dot3_left(_tri(L, "lower"), gcol)
    brow_all = _dot3_right(grow, _tri(L, "upper"))
    causal = _iota((L, L), 0) >= _iota((L, L), 1)
    ones = jnp.ones((L, MLSTM_DV), BF16)
    for h in range(MLSTM_HEADS):
        q = q_ref[:, h * MLSTM_DK:(h + 1) * MLSTM_DK]
        k = k_ref[:, h * MLSTM_DK:(h + 1) * MLSTM_DK]
        v1 = jnp.concatenate([v_ref[:, h * MLSTM_DV:(h + 1) * MLSTM_DV], ones], axis=1)
        igc, bc = gcol[:, h:h + 1], bcol_all[:, MLSTM_HEADS + h:MLSTM_HEADS + h + 1]
        igr, br = grow[h:h + 1, :], brow_all[MLSTM_HEADS + h:MLSTM_HEADS + h + 1, :]
        mp = ms_ref[h:h + 1, 0:1]
        d = jnp.where(causal, bc - br + igr, NEG)
        inter = bc + mp
        mt = jnp.maximum(inter, jnp.max(d, axis=1, keepdims=True))
        wi = jnp.exp(d - mt)
        we = jnp.exp(inter - mt)
        s = _dot_nt(q, k) * wi
        sc = sc_ref[h]
        qs = _dot(q, sc)
        sv = jnp.dot(s.astype(BF16), v1, preferred_element_type=F32)
        num = we * qs[:, 0:MLSTM_DV] + sv[:, 0:MLSTM_DV]
        den = we * qs[:, MLSTM_DV:MLSTM_DV + 1] + sv[:, MLSTM_DV:MLSTM_DV + 1]
        hh = num / jnp.maximum(jnp.abs(den), jnp.exp(-mt))
        hh = hh * lax.rsqrt(jnp.mean(hh * hh, axis=-1, keepdims=True) + EPS)
        sl = slice(h * MLSTM_DV, (h + 1) * MLSTM_DV)
        hm_ref[:, sl] = (hh * outg_ref[:, sl] * _sigmoid(o_ref[:, sl])).astype(BF16)
        bl = bc[L - 1:L, :]
        gr = bl - br + igr
        gcl = bl - bc + igc
        mn = jnp.maximum(bl + mp, jnp.max(gr, axis=1, keepdims=True))
        wk = jnp.exp(gcl - mn)
        dec = jnp.exp(bl + mp - mn)
        sc_ref[h] = dec * sc + _dot_tn(k.astype(F32) * wk, v1)
        ms_ref[h:h + 1, :] = jnp.broadcast_to(mn, (1, LANES))

    @pl.when(ci == pl.num_programs(1) - 1)
    def _():
        c_out[...] = sc_ref[:, :, 0:MLSTM_DV]
        n_out[...] = sc_ref[:, :, MLSTM_DV:]
        m_out[...] = ms_ref[...]


def _mlstm_chunk(mq, mk, mv, mo, gcol, grow, outg, c0, n0rep, m0rep, b, t):
    L = min(MLSTM_CHUNK, t)
    nc = t // L
    row = lambda c: pl.BlockSpec((L, c), lambda bi, ci: (bi * nc + ci, 0))
    st = lambda *s: pl.BlockSpec((None,) + s, lambda bi, ci: (bi,) + (0,) * len(s))
    H, DK, DV = MLSTM_HEADS, MLSTM_DK, MLSTM_DV
    return pl.pallas_call(
        _mlstm_chunk_kernel, grid=(b, nc),
        in_specs=[row(256), row(256), row(512), row(512), row(LANES),
                  pl.BlockSpec((8, L), lambda bi, ci: (0, bi * nc + ci)),
                  _const_spec((1, H * DV)), st(H, DK, DV), st(H, DK, DV), st(8, LANES)],
        out_specs=[row(H * DV), st(H, DK, DV), st(H, DK, DV), st(8, LANES)],
        out_shape=[jax.ShapeDtypeStruct((b * t, H * DV), BF16),
                   jax.ShapeDtypeStruct((b, H, DK, DV), F32),
                   jax.ShapeDtypeStruct((b, H, DK, DV), F32),
                   jax.ShapeDtypeStruct((b, 8, LANES), F32)],
        scratch_shapes=[pltpu.VMEM((H, DK, 2 * DV), F32), pltpu.VMEM((8, LANES), F32)],
        compiler_params=_cparams(("parallel", "arbitrary")),
    )(mq, mk, mv, mo, gcol, grow, outg, c0, n0rep, m0rep)


def _ffn_tail(x, mixed, g_ref, wgate_ref, wup_ref, wdown_ref, o_ref, n_chunks):
    x1 = x + _rms(mixed, g_ref[0:1, :])
    h = _rms(x1, g_ref[1:2, :]).astype(BF16)
    ch = wgate_ref.shape[1] // n_chunks
    y = jnp.zeros(x.shape, F32)
    for c in range(n_chunks):
        sl = slice(c * ch, (c + 1) * ch)
        a = jnp.dot(h, wgate_ref[:, sl], preferred_element_type=F32)
        u = jnp.dot(h, wup_ref[:, sl], preferred_element_type=F32)
        act = (a * _sigmoid(a) * u).astype(BF16)
        y = y + jnp.dot(act, wdown_ref[sl, :], preferred_element_type=F32)
    o_ref[...] = x1 + _rms(y, g_ref[2:3, :])


def _hyb_out_kernel(x_ref, hm_ref, hf_ref, wo_ref, g_ref, wgate_ref, wup_ref, wdown_ref, o_ref, *, n_chunks):
    half = hm_ref.shape[1]
    mixed = (jnp.dot(hm_ref[...], wo_ref[0:half, :], preferred_element_type=F32)
             + jnp.dot(hf_ref[...], wo_ref[half:, :], preferred_element_type=F32))
    _ffn_tail(x_ref[...], mixed, g_ref, wgate_ref, wup_ref, wdown_ref, o_ref, n_chunks)


def _seg_sum(x, bd):
    hi = x.astype(BF16)
    lo = (x - hi.astype(F32)).astype(BF16)
    parts = []
    for p in range(x.shape[1] // LANES):
        sl = slice(p * LANES, (p + 1) * LANES)
        parts.append(jnp.dot(hi[:, sl], bd, preferred_element_type=F32)
                     + jnp.dot(lo[:, sl], bd, preferred_element_type=F32))
    return jnp.concatenate(parts, axis=1)


def _block_diag_ones():
    r, c = _iota((LANES, LANES), 0), _iota((LANES, LANES), 1)
    return jnp.where((r // RWKV_DH) == (c // RWKV_DH), 1.0, 0.0).astype(BF16)


def _rwkv_out_kernel(x_ref, y_ref, r_ref, k_ref, v_ref, gate_ref, vec_ref, wo_ref, g_ref,
                     wgate_ref, wup_ref, wdown_ref, o_ref, *, n_chunks):
    bd = _block_diag_ones()
    y = y_ref[...]
    inv = 1.0 / RWKV_DH
    mu = _seg_sum(y, bd) * inv
    yc = y - mu
    var = _seg_sum(yc * yc, bd) * inv
    yn = yc * lax.rsqrt(var + GN_EPS) * vec_ref[0:1, :] + vec_ref[1:2, :]
    r = r_ref[...].astype(F32)
    bonus = _seg_sum(r * k_ref[...].astype(F32) * vec_ref[2:3, :], bd) * v_ref[...].astype(F32)
    z = ((yn + bonus) * gate_ref[...].astype(F32)).astype(BF16)
    mixed = jnp.dot(z, wo_ref[...], preferred_element_type=F32)
    _ffn_tail(x_ref[...], mixed, g_ref, wgate_ref, wup_ref, wdown_ref, o_ref, n_chunks)


def _ffn_chunks(hidden):
    for n in (2, 4, 1):
        if hidden % (n * LANES) == 0:
            return n
    return 1


def _hyb_out(x, hm, hf, wo, g3, wgate, wup, wdown, tm):
    n, d = x.shape
    row = lambda c: pl.BlockSpec((tm, c), lambda i: (i, 0))
    return pl.pallas_call(
        functools.partial(_hyb_out_kernel, n_chunks=_ffn_chunks(wgate.shape[1])), grid=(n // tm,),
        in_specs=[row(d), row(hm.shape[1]), row(hf.shape[1]), _const_spec(wo.shape), _const_spec(g3.shape),
                  _const_spec(wgate.shape), _const_spec(wup.shape), _const_spec(wdown.shape)],
        out_specs=row(d), out_shape=jax.ShapeDtypeStruct((n, d), F32),
        compiler_params=_cparams(("parallel",)),
    )(x, hm, hf, wo, g3, wgate, wup, wdown)


def _rwkv_out(x, y, r, k, v, gate, vec, wo, g3, wgate, wup, wdown, tm):
    n, d = x.shape
    row = lambda c: pl.BlockSpec((tm, c), lambda i: (i, 0))
    return pl.pallas_call(
        functools.partial(_rwkv_out_kernel, n_chunks=_ffn_chunks(wgate.shape[1])), grid=(n // tm,),
        in_specs=[row(d)] * 6 + [_const_spec(vec.shape), _const_spec(wo.shape), _const_spec(g3.shape),
                                 _const_spec(wgate.shape), _const_spec(wup.shape), _const_spec(wdown.shape)],
        out_specs=row(d), out_shape=jax.ShapeDtypeStruct((n, d), F32),
        compiler_params=_cparams(("parallel",)),
    )(x, y, r, k, v, gate, vec, wo, g3, wgate, wup, wdown)


def _rwkv_proj_body(h, hprev, vec_ref, wr_ref, wk_ref, wv_ref, w1_ref, w2_ref, a1_ref, a2_ref, g1_ref, g2_ref,
                    r_ref, k_ref, v_ref, na_ref, bb_ref, lw_ref, gate_ref):
    xx = hprev - h
    mixed = lambda i: (h + xx * vec_ref[i:i + 1, :]).astype(BF16)
    r = jnp.dot(mixed(0), wr_ref[...], preferred_element_type=F32)
    k = jnp.dot(mixed(2), wk_ref[...], preferred_element_type=F32)
    v = jnp.dot(mixed(3), wv_ref[...], preferred_element_type=F32)
    wl = vec_ref[6:7, :] + _dot(jnp.tanh(jnp.dot(mixed(1), w1_ref[...], preferred_element_type=F32)), w2_ref[...])
    w_log = -_softplus(-wl) - 0.5
    a = _sigmoid(vec_ref[7:8, :] + _dot(jnp.dot(mixed(4), a1_ref[...], preferred_element_type=F32), a2_ref[...]))
    gate = _dot(_sigmoid(jnp.dot(mixed(5), g1_ref[...], preferred_element_type=F32)), g2_ref[...])
    kk = k * vec_ref[8:9, :]
    nrm = jnp.sqrt(_seg_sum(kk * kk, _block_diag_ones()))
    kk = kk / jnp.maximum(nrm, 1e-12)
    k2 = k * (1.0 + (a - 1.0) * vec_ref[9:10, :])
    r_ref[...] = r.astype(BF16)
    k_ref[...] = k2.astype(BF16)
    v_ref[...] = v.astype(BF16)
    na_ref[...] = (-kk).astype(BF16)
    bb_ref[...] = (kk * a).astype(BF16)
    lw_ref[...] = -jnp.exp(w_log)
    gate_ref[...] = gate.astype(BF16)


def _rwkv_proj_seq_kernel(x_ref, xprev_ref, shift_ref, g_ref, vec_ref, *rest):
    weights, outs = rest[:9], rest[9:]
    hlast_ref = outs[-1]
    g = g_ref[...]
    h = _rms(x_ref[...], g)
    tm = h.shape[0]
    prev_tile_last = _rms(xprev_ref[...], g)[7:8, :]
    first = jnp.where(pl.program_id(1) == 0, shift_ref[...], prev_tile_last)
    hprev = jnp.where(_iota(h.shape, 0) == 0, first, pltpu.roll(h, 1, 0))
    _rwkv_proj_body(h, hprev, vec_ref, *weights, *outs[:-1])
    hlast_ref[...] = h[tm - 8:tm, :]


def _rwkv_proj_step_kernel(x_ref, shift_ref, g_ref, vec_ref, *rest):
    weights, outs = rest[:9], rest[9:]
    h = _rms(x_ref[...], g_ref[...])
    _rwkv_proj_body(h, shift_ref[...], vec_ref, *weights, *outs[:-1])
    outs[-1][...] = h


_RWKV_PROJ_OUT = (BF16, BF16, BF16, BF16, BF16, F32, BF16)


def _rwkv_proj_seq(x, shift0, g, vec, weights, b, t, tm):
    n, d = x.shape
    nt = t // tm
    row = pl.BlockSpec((tm, d), lambda bi, i: (bi * nt + i, 0))
    prev = pl.BlockSpec((8, d), lambda bi, i: (jnp.maximum((bi * nt + i) * (tm // 8) - 1, 0), 0))
    return pl.pallas_call(
        _rwkv_proj_seq_kernel, grid=(b, nt),
        in_specs=[row, prev, pl.BlockSpec((None, 1, d), lambda bi, i: (bi, 0, 0)), _const_spec((1, d)),
                  _const_spec(vec.shape)] + [_const_spec(w.shape) for w in weights],
        out_specs=[row] * 7 + [pl.BlockSpec((None, 8, d), lambda bi, i: (bi, 0, 0))],
        out_shape=[jax.ShapeDtypeStruct((n, d), dt) for dt in _RWKV_PROJ_OUT]
        + [jax.ShapeDtypeStruct((b, 8, d), F32)],
        compiler_params=_cparams(("parallel", "arbitrary")),
    )(x, x, shift0.reshape(b, 1, d), g, vec, *weights)


def _rwkv_proj_step(x, shift0, g, vec, weights):
    n, d = x.shape
    full = pl.BlockSpec((n, d), lambda i: (0, 0))
    return pl.pallas_call(
        _rwkv_proj_step_kernel, grid=(1,),
        in_specs=[full, full, _const_spec((1, d)), _const_spec(vec.shape)] + [_const_spec(w.shape) for w in weights],
        out_specs=[full] * 8,
        out_shape=[jax.ShapeDtypeStruct((n, d), dt) for dt in _RWKV_PROJ_OUT + (F32,)],
        compiler_params=_cparams(("arbitrary",)),
    )(x, shift0, g, vec, *weights)


def _unit_lower_inverse(nmat):
    n = nmat.shape[0]
    r, c = _iota((n, n), 0), _iota((n, n), 1)
    eye = jnp.where(r == c, 1.0, 0.0)
    t = eye + jnp.where((r // 2 == c // 2), nmat, 0.0)
    bsz = 2
    while bsz < n // 2:
        sel = (r // (2 * bsz) == c // (2 * bsz)) & (r // bsz != c // bsz)
        off = jnp.where(sel, nmat, 0.0)
        t = t + _dot(t, _dot(off, t))
        bsz *= 2
    return t


def _rwkv_chunk_kernel(r_ref, k_ref, v_ref, na_ref, bb_ref, lw_ref, s0_ref, y_ref, s_out, st_ref):
    L = r_ref.shape[0]
    ci = pl.program_id(1)

    @pl.when(ci == 0)
    def _():
        st_ref[...] = s0_ref[...]

    lw = lw_ref[...]
    cw = _dot3_left(_tri(L, "lower"), lw)
    n2 = 2 * L
    rr, cc = _iota((n2, n2), 0), _iota((n2, n2), 1)
    strict = rr > cc
    incl = rr >= cc
    lane = _iota((L, LANES), 1)
    m0 = lane < RWKV_DH
    stack = lambda x: jnp.concatenate([jnp.where(m0, x, 0.0), jnp.where(m0, 0.0, x)], axis=0)
    for p in range(r_ref.shape[1] // LANES):
        sl = slice(p * LANES, (p + 1) * LANES)
        cwp, lwp = cw[:, sl], lw[:, sl]
        wl = cwp[L - 1:L, :]
        w_t = jnp.exp(cwp)
        w_inv = jnp.exp(-cwp)
        w_prev = jnp.exp(cwp - lwp)
        w_end = jnp.exp(wl - cwp)
        r, k, v = r_ref[:, sl].astype(F32), k_ref[:, sl].astype(F32), v_ref[:, sl].astype(F32)
        na, bb = na_ref[:, sl].astype(F32), bb_ref[:, sl].astype(F32)
        xs = jnp.concatenate([stack(na * w_prev), stack(r * w_t)], axis=0).astype(BF16)
        zs = jnp.concatenate([stack(bb * w_inv), stack(k * w_inv)], axis=0).astype(BF16)
        ze = jnp.concatenate([stack(bb * w_end), stack(k * w_end)], axis=0).astype(BF16)
        vs = stack(v).astype(BF16)
        g = _dot_nt(xs, zs)
        nmat = jnp.where(strict, g[0:n2, 0:n2], 0.0)
        a_ak = jnp.where(strict, g[0:n2, n2:], 0.0)
        a_rb = jnp.where(incl, g[n2:, 0:n2], 0.0)
        a_rk = jnp.where(incl, g[n2:, n2:], 0.0)
        st = st_ref[p]
        xst = _dot_nt(xs, st)
        av = jnp.dot(jnp.concatenate([a_ak, a_rk], axis=0).astype(BF16), vs, preferred_element_type=F32)
        u = _dot(_unit_lower_inverse(nmat), xst[0:n2] + av[0:n2])
        ys = xst[n2:] + _dot(a_rb, u) + av[n2:]
        y_ref[:, sl] = ys[0:L] + ys[L:]
        uv = jnp.concatenate([u.astype(BF16), vs], axis=0)
        st_ref[p] = st * jnp.exp(wl) + _dot_tn(uv, ze)

    @pl.when(ci == pl.num_programs(1) - 1)
    def _():
        s_out[...] = st_ref[...]


def _rwkv_chunk(r, k, v, na, bb, lw, s0bd, b, t):
    L = min(RWKV_CHUNK, t)
    nc = t // L
    d = r.shape[1]
    pairs = d // LANES
    row = pl.BlockSpec((L, d), lambda bi, ci: (bi * nc + ci, 0))
    st = pl.BlockSpec((None, pairs, LANES, LANES), lambda bi, ci: (bi, 0, 0, 0))
    return pl.pallas_call(
        _rwkv_chunk_kernel, grid=(b, nc),
        in_specs=[row] * 6 + [st], out_specs=[row, st],
        out_shape=[jax.ShapeDtypeStruct((b * t, d), F32), jax.ShapeDtypeStruct((b, pairs, LANES, LANES), F32)],
        scratch_shapes=[pltpu.VMEM((pairs, LANES, LANES), F32)],
        compiler_params=_cparams(("parallel", "arbitrary")),
    )(r, k, v, na, bb, lw, s0bd)


def _fox_decode_kernel(pt_ref, q_ref, knew_ref, vnew_ref, lfnew_ref, *rest, n_pages):
    del pt_ref
    k_refs, v_refs, lf_refs = rest[0:n_pages], rest[n_pages:2 * n_pages], rest[2 * n_pages:3 * n_pages]
    o_ref = rest[3 * n_pages]
    width = FOX_HEADS * FOX_DH
    hmask = (_iota((FOX_HEADS, width), 1) // FOX_DH) == _iota((FOX_HEADS, width), 0)
    qbd = jnp.where(hmask, jnp.broadcast_to(q_ref[...].astype(F32), (FOX_HEADS, width)), 0.0).astype(BF16)
    page = k_refs[0].shape[0]
    later = _tri(page, "strict_upper_t")
    run = lfnew_ref[...]
    logits = [None] * n_pages
    for j in reversed(range(n_pages)):
        lf = lf_refs[j][...]
        logits[j] = _dot_nt(qbd, k_refs[j][...]) + _dot3_right(lf, later) + run
        run = run + jnp.sum(lf, axis=1, keepdims=True)
    s_self = _dot_nt(qbd, jnp.broadcast_to(knew_ref[...], (8, width)))[:, 0:1]
    m = s_self
    for j in range(n_pages):
        m = jnp.maximum(m, jnp.max(logits[j], axis=1, keepdims=True))
    p_self = jnp.exp(s_self - m)
    l = p_self
    acc = p_self * vnew_ref[...]
    for j in range(n_pages):
        p = jnp.exp(logits[j] - m)
        l = l + jnp.sum(p, axis=1, keepdims=True)
        acc = acc + _dot(p, v_refs[j][...])
    o_ref[...] = jnp.sum(jnp.where(hmask, acc / l, 0.0), axis=0, keepdims=True)


def _fox_decode(page_table, q, knew, vnew, lfnew, cache_k, cache_v, cache_lf_t):
    nb, n_pages = page_table.shape
    width = q.shape[1]
    page = cache_k.shape[1]
    row = pl.BlockSpec((None, 1, width), lambda b, pt: (b, 0, 0))
    paged = lambda j, shape: pl.BlockSpec((None,) + shape, lambda b, pt, j=j: (pt[b * n_pages + j], 0, 0))
    in_specs = ([row, row, row, pl.BlockSpec((None, FOX_HEADS, 1), lambda b, pt: (b, 0, 0))]
                + [paged(j, (page, width)) for j in range(n_pages)] * 1
                + [paged(j, (page, width)) for j in range(n_pages)]
                + [paged(j, (FOX_HEADS, page)) for j in range(n_pages)])
    out = pl.pallas_call(
        functools.partial(_fox_decode_kernel, n_pages=n_pages),
        grid_spec=pltpu.PrefetchScalarGridSpec(
            num_scalar_prefetch=1, grid=(nb,), in_specs=in_specs,
            out_specs=pl.BlockSpec((None, 1, width), lambda b, pt: (b, 0, 0))),
        out_shape=jax.ShapeDtypeStruct((nb, 1, width), F32),
        compiler_params=_cparams(("arbitrary",)),
    )(page_table.reshape(-1), q.reshape(nb, 1, width), knew.reshape(nb, 1, width), vnew.reshape(nb, 1, width),
      lfnew.reshape(nb, FOX_HEADS, 1), *([cache_k] * n_pages), *([cache_v] * n_pages), *([cache_lf_t] * n_pages))
    return out.reshape(nb, width)


def _columns(rows):
    n = rows.shape[0]
    hi, mid, lo = _split3(rows)
    r, c = _iota((RWKV_DH, RWKV_DH), 0), _iota((RWKV_DH, RWKV_DH), 1)
    eye = jnp.where(r == c, 1.0, 0.0).astype(BF16)
    t = lax.dot_general(eye, jnp.concatenate([hi, mid, lo], axis=0), (((1,), (1,)), ((), ())),
                        preferred_element_type=F32)
    return (t[:, 2 * n:3 * n] + t[:, n:2 * n]) + t[:, 0:n]


def _mlstm_step_kernel(q_ref, k_ref, v_ref, o_ref, gs_ref, m0_ref, outg_ref, c0_ref, n0_ref,
                       hm_ref, c_out, n_out, m_out):
    kf = k_ref[...].astype(F32)
    kcols = _columns(jnp.concatenate([kf, jnp.zeros((8 - MLSTM_HEADS, MLSTM_DK), F32)], axis=0))
    gs = gs_ref[...]
    m0 = m0_ref[...]
    for h in range(MLSTM_HEADS):
        q, k, v = q_ref[h:h + 1, :].astype(F32), kf[h:h + 1, :], v_ref[h:h + 1, :].astype(F32)
        ig, lf, mp = gs[:, h:h + 1], gs[:, MLSTM_HEADS + h:MLSTM_HEADS + h + 1], m0[:, h:h + 1]
        inter = lf + mp
        mt = jnp.maximum(inter, ig)
        wi = jnp.exp(ig - mt)
        we = jnp.exp(inter - mt)
        s = jnp.sum(q * k, axis=1, keepdims=True) * wi
        c = c0_ref[h]
        n = n0_ref[h:h + 1, :]
        qc = _dot(jnp.broadcast_to(q, (8, MLSTM_DK)), c)[0:1, :]
        num = we * qc + s * v
        den = we * jnp.sum(q * n, axis=1, keepdims=True) + s
        hh = num / jnp.maximum(jnp.abs(den), jnp.exp(-mt))
        hh = hh * lax.rsqrt(jnp.mean(hh * hh, axis=-1, keepdims=True) + EPS)
        sl = slice(h * MLSTM_DV, (h + 1) * MLSTM_DV)
        hm_ref[:, sl] = hh * outg_ref[:, sl] * _sigmoid(o_ref[:, sl])
        c_out[h] = we * c + (wi * kcols[:, h:h + 1]) * v
        n_out[h:h + 1, :] = we * n + wi * k
        m_out[:, h:h + 1] = mt


def _mlstm_step(mq, mk, mv, mo, gcol, outg, c0, n0, m0):
    nb = mq.shape[0]
    H, DK, DV = MLSTM_HEADS, MLSTM_DK, MLSTM_DV
    per = lambda *s: pl.BlockSpec((None,) + s, lambda b: (b,) + (0,) * len(s))
    hm, c, n, m = pl.pallas_call(
        _mlstm_step_kernel, grid=(nb,),
        in_specs=[per(H, DK), per(H, DK), per(H, DV), per(1, H * DV), per(1, LANES), per(1, H),
                  _const_spec((1, H * DV)), per(H, DK, DV), per(H, DK)],
        out_specs=[per(1, H * DV), per(H, DK, DV), per(H, DK), per(1, H)],
        out_shape=[jax.ShapeDtypeStruct((nb, 1, H * DV), F32), jax.ShapeDtypeStruct((nb, H, DK, DV), F32),
                   jax.ShapeDtypeStruct((nb, H, DK), F32), jax.ShapeDtypeStruct((nb, 1, H), F32)],
        compiler_params=_cparams(("arbitrary",)),
    )(mq.reshape(nb, H, DK), mk.reshape(nb, H, DK), mv.reshape(nb, H, DV), mo.reshape(nb, 1, H * DV),
      gcol.reshape(nb, 1, LANES), m0.reshape(nb, 1, H), outg, c0, n0)
    return hm.reshape(nb, H * DV), c, n, m.reshape(nb, H)


def _rwkv_step_kernel(r_ref, k_ref, v_ref, na_ref, bb_ref, lw_ref, s0_ref, y_ref, s_out):
    r, k, v = r_ref[...].astype(F32), k_ref[...].astype(F32), v_ref[...].astype(F32)
    na, bb = na_ref[...].astype(F32), bb_ref[...].astype(F32)
    w = jnp.exp(lw_ref[...])
    vcols = _columns(v)
    br = jnp.sum(bb * r, axis=1, keepdims=True)
    kr = jnp.sum(k * r, axis=1, keepdims=True)
    wr = w * r
    zeros = jnp.zeros((6, RWKV_DH), F32)
    for h in range(r.shape[0]):
        s = s0_ref[h]
        rows = jnp.concatenate([na[h:h + 1], wr[h:h + 1], zeros], axis=0)
        sa_col = _dot_nt(s, rows)[:, 0:1]
        as_rows = _dot_nt(rows, s)
        s_out[h] = s * w[h:h + 1] + sa_col * bb[h:h + 1] + vcols[:, h:h + 1] * k[h:h + 1]
        y_ref[h:h + 1, :] = as_rows[1:2] + as_rows[0:1] * br[h:h + 1] + v[h:h + 1] * kr[h:h + 1]


def _rwkv_step(r, k, v, na, bb, lw, s0):
    nb, d = r.shape
    H, N = RWKV_HEADS, RWKV_DH
    per = lambda *s: pl.BlockSpec((None,) + s, lambda b: (b,) + (0,) * len(s))
    rs = lambda a: a.reshape(nb, H, N)
    y, s = pl.pallas_call(
        _rwkv_step_kernel, grid=(nb,),
        in_specs=[per(H, N)] * 6 + [per(H, N, N)],
        out_specs=[per(H, N), per(H, N, N)],
        out_shape=[jax.ShapeDtypeStruct((nb, H, N), F32), jax.ShapeDtypeStruct((nb, H, N, N), F32)],
        compiler_params=_cparams(("arbitrary",)),
    )(rs(r), rs(k), rs(v), rs(na), rs(bb), rs(lw), s0)
    return y.reshape(nb, d), s


def _pair_block_diag(s):
    b, h, n, _ = s.shape
    s = s.reshape(b, h // 2, 2, n, n)
    z = jnp.zeros_like(s[:, :, 0])
    top = jnp.concatenate([s[:, :, 0], z], axis=-1)
    bot = jnp.concatenate([z, s[:, :, 1]], axis=-1)
    return jnp.concatenate([top, bot], axis=-2)


def _pair_block_diag_inv(sbd, n):
    return jnp.stack([sbd[:, :, 0:n, 0:n], sbd[:, :, n:, n:]], axis=2).reshape(sbd.shape[0], -1, n, n)


def kernel(x_prompt, x_sample, cache_fox_k, cache_fox_v, cache_fox_lf, state_mlstm_c, state_mlstm_n, state_mlstm_m, state_rwkv_s, state_rwkv_shift, page_table, norm_g, hyb_w_in, mlstm_i_bias, mlstm_f_bias, mlstm_out_g, fox_f_bias, hyb_w_out, rwkv_mix, rwkv_w_r, rwkv_w_k, rwkv_w_v, rwkv_w_o, rwkv_w0, rwkv_w1, rwkv_w2, rwkv_a0, rwkv_a1, rwkv_a2, rwkv_g1, rwkv_g2, rwkv_k_k, rwkv_k_a, rwkv_r_k, rwkv_ln_w, rwkv_ln_b, ffn_w_gate, ffn_w_up, ffn_w_down):
    bp, t, d = x_prompt.shape
    bs = x_sample.shape[0]
    assert x_sample.shape[1] == 1 and norm_g.shape[0] == 2
    n_phys, page = cache_fox_k.shape[1], cache_fox_k.shape[2]
    H, DK, DV = MLSTM_HEADS, MLSTM_DK, MLSTM_DV
    b16 = lambda a: a.astype(BF16)

    w_in = hyb_w_in[0]
    g_lo = 2 * H * DK + 2 * H * DV
    w_main = b16(jnp.concatenate([w_in[:, :g_lo], w_in[:, g_lo + 2 * H:g_lo + 2 * H + 3 * FOX_HEADS * FOX_DH]], axis=1))
    w_gate_cols = jnp.concatenate([w_in[:, g_lo:g_lo + 2 * H], w_in[:, -FOX_HEADS:]], axis=1)
    n_g = w_gate_cols.shape[1]
    w_g = b16(jnp.pad(w_gate_cols, ((0, 0), (0, LANES - n_g))))
    gate_bias = jnp.pad(jnp.concatenate([mlstm_i_bias[0], mlstm_f_bias[0], fox_f_bias[0]]), (0, LANES - n_g))[None, :]
    outg = mlstm_out_g[0][None, :]
    w_out = b16(hyb_w_out[0])
    ffn = [(b16(ffn_w_gate[l]), b16(ffn_w_up[l]), b16(ffn_w_down[l])) for l in range(2)]
    rw = [b16(a[0]) for a in (rwkv_w_r, rwkv_w_k, rwkv_w_v, rwkv_w1, rwkv_w2, rwkv_a1, rwkv_a2, rwkv_g1, rwkv_g2)]
    proj_vec = jnp.concatenate([rwkv_mix[0], rwkv_w0, rwkv_a0, rwkv_k_k, rwkv_k_a,
                                jnp.zeros((6, d), F32)], axis=0)
    out_vec = jnp.concatenate([rwkv_ln_w, rwkv_ln_b, rwkv_r_k.reshape(1, d), jnp.zeros((5, d), F32)], axis=0)
    w_o = b16(rwkv_w_o[0])

    def even_in(x, tm, with_rows):
        return _hyb_in(x, norm_g[0, 0][None, :], w_main, w_g, gate_bias, tm, with_rows)

    n = bp * t
    tm = min(ROW_TILE, t)
    xp = x_prompt.reshape(n, d)
    mq, mk, mv, mo, fq, fk, fv, fk16, fv16, gcol, grow = even_in(xp, tm, True)
    q_aug, k_aug = _fox_prep(fq, fk16, gcol, bp, t, tm)
    hf = _fox_flash(q_aug, k_aug, fv16, bp, t, min(ATTN_BLOCK, t))
    zc = jnp.zeros((bp, H, DK, DV), F32)
    hm, pc, pn_rep, pm_rep = _mlstm_chunk(mq, mk, mv, mo, gcol, grow, outg, zc, zc,
                                          jnp.zeros((bp, 8, LANES), F32), bp, t)
    x1 = _hyb_out(xp, hm, hf, w_out, norm_g[0, 1:4], *ffn[0], tm)
    r, k2, v, na, bb, lw, gate, hlast = _rwkv_proj_seq(x1, jnp.zeros((bp, d), F32), norm_g[1, 0][None, :],
                                                       proj_vec, rw, bp, t, tm)
    s0p = jnp.zeros((bp, RWKV_HEADS // 2, LANES, LANES), F32)
    y, ps_bd = _rwkv_chunk(r, k2, v, na, bb, lw, s0p, bp, t)
    y_prompt = _rwkv_out(x1, y, r, k2, v, gate, out_vec, w_o, norm_g[1, 1:4], *ffn[1], tm).reshape(bp, t, d)
    p_rwkv_s = _pair_block_diag_inv(ps_bd, RWKV_DH)

    xs = x_sample.reshape(bs, d)
    smq, smk, smv, smo, sfq, sfk, sfv, _, _, sgcol = even_in(xs, bs, False)
    lff_new = sgcol[:, 2 * H:2 * H + FOX_HEADS]
    shf = _fox_decode(page_table, sfq, sfk, sfv, lff_new,
                      cache_fox_k[0].reshape(n_phys, page, -1), cache_fox_v[0].reshape(n_phys, page, -1),
                      jnp.swapaxes(cache_fox_lf[0], 1, 2))
    shm, sc, sn, sm = _mlstm_step(smq, smk, smv, smo, sgcol, outg, state_mlstm_c[0], state_mlstm_n[0],
                                  state_mlstm_m[0])
    xs1 = _hyb_out(xs, b16(shm), b16(shf), w_out, norm_g[0, 1:4], *ffn[0], bs)
    sr, sk2, sv, sna, sbb, slw, sgate, sh = _rwkv_proj_step(xs1, state_rwkv_shift[0], norm_g[1, 0][None, :],
                                                            proj_vec, rw)
    sy, ss = _rwkv_step(sr, sk2, sv, sna, sbb, slw, state_rwkv_s[0])
    y_sample = _rwkv_out(xs1, sy, sr, sk2, sv, sgate, out_vec, w_o, norm_g[1, 1:4], *ffn[1], bs).reshape(bs, 1, d)

    e = lambda a: a[None]
    return (y_prompt, y_sample,
            e(fk.reshape(bp, t, FOX_HEADS, FOX_DH)), e(fv.reshape(bp, t, FOX_HEADS, FOX_DH)),
            e(gcol[:, 2 * H:2 * H + FOX_HEADS].reshape(bp, t, FOX_HEADS)),
            e(pc), e(pn_rep[..., 0]), e(pm_rep[:, :H, 0]), e(p_rwkv_s), e(hlast[:, 7, :]),
            e(sfk.reshape(bs, 1, FOX_HEADS, FOX_DH)), e(sfv.reshape(bs, 1, FOX_HEADS, FOX_DH)),
            e(lff_new.reshape(bs, 1, FOX_HEADS)),
            e(sc), e(sn), e(sm), e(ss), e(sh))
```

```python
import functools

import jax
import jax.numpy as jnp
import numpy as np
from jax import lax
from jax.experimental import pallas as pl
from jax.experimental.pallas import tpu as pltpu

F32, BF16 = jnp.float32, jnp.bfloat16
EPS = 1e-6
GN_EPS = 64e-5
NEG = -1e30
LOG2E = 1.4426950408889634

MLSTM_HEADS, MLSTM_DK, MLSTM_DV = 4, 64, 128
FOX_HEADS, FOX_DH = 8, 64
RWKV_HEADS, RWKV_DH = 16, 64
MLSTM_CHUNK = 128
RWKV_CHUNK = 64
LANES = 128
ATTN_BLOCK = 512
ROW_TILE = 512
VMEM_LIMIT = 56 * 1024 * 1024


def _cparams(sem):
    return pltpu.CompilerParams(dimension_semantics=sem, vmem_limit_bytes=VMEM_LIMIT)


def _const_spec(shape):
    nd = len(shape)
    return pl.BlockSpec(shape, lambda *a: (0,) * nd, pipeline_mode=pl.Buffered(1))


def _dot(a, b):
    return jnp.dot(a.astype(BF16), b.astype(BF16), preferred_element_type=F32)


def _dot_nt(a, b):
    return lax.dot_general(a.astype(BF16), b.astype(BF16), (((1,), (1,)), ((), ())),
                           preferred_element_type=F32)


def _dot_tn(a, b):
    return lax.dot_general(a.astype(BF16), b.astype(BF16), (((0,), (0,)), ((), ())),
                           preferred_element_type=F32)


def _split3(x):
    hi = x.astype(BF16)
    r1 = x - hi.astype(F32)
    mid = r1.astype(BF16)
    lo = (r1 - mid.astype(F32)).astype(BF16)
    return hi, mid, lo


def _dot3_left(m, x):
    hi, mid, lo = _split3(x)
    d = lambda t: jnp.dot(m, t, preferred_element_type=F32)
    return (d(lo) + d(mid)) + d(hi)


def _dot3_right(x, m):
    hi, mid, lo = _split3(x)
    d = lambda t: jnp.dot(t, m, preferred_element_type=F32)
    return (d(lo) + d(mid)) + d(hi)


def _rms(x, g):
    return x * lax.rsqrt(jnp.mean(x * x, axis=-1, keepdims=True) + EPS) * g


def _sigmoid(x):
    return 1.0 / (1.0 + jnp.exp(-x))


def _log_sigmoid(x):
    return jnp.minimum(x, 0.0) - jnp.log1p(jnp.exp(-jnp.abs(x)))


def _softplus(x):
    return jnp.maximum(x, 0.0) + jnp.log1p(jnp.exp(-jnp.abs(x)))


def _iota(shape, dim):
    return lax.broadcasted_iota(jnp.int32, shape, dim)


def _tri(n, kind):
    r, c = _iota((n, n), 0), _iota((n, n), 1)
    m = {"lower": r >= c, "upper": r <= c, "strict_upper_t": r > c}[kind]
    return jnp.where(m, 1.0, 0.0).astype(BF16)


def _hyb_in_kernel(x_ref, g_ref, w_ref, wg_ref, bias_ref,
                   mq_ref, mk_ref, mv_ref, mo_ref, fq_ref, fk_ref, fv_ref, fk16_ref, fv16_ref,
                   gcol_ref, *maybe_grow):
    hn = _rms(x_ref[...], g_ref[...]).astype(BF16)
    seg = lambda lo, hi: jnp.dot(hn, w_ref[:, lo:hi], preferred_element_type=F32)
    mq_ref[...] = seg(0, 256).astype(BF16)
    mk_ref[...] = (seg(256, 512) * (MLSTM_DK ** -0.5)).astype(BF16)
    mv_ref[...] = seg(512, 1024).astype(BF16)
    mo_ref[...] = seg(1024, 1536)
    q_scale = FOX_DH ** -0.5 * (LOG2E if maybe_grow else 1.0)
    fq_ref[...] = (seg(1536, 2048) * q_scale).astype(BF16)
    fk = seg(2048, 2560)
    fv = seg(2560, 3072)
    fk_ref[...] = fk.T if maybe_grow else fk
    fv_ref[...] = fv.T if maybe_grow else fv
    fk16_ref[...] = fk.astype(BF16)
    fv16_ref[...] = (fv.T if maybe_grow else fv).astype(BF16)
    gates = jnp.dot(hn, wg_ref[...], preferred_element_type=F32) + bias_ref[...]
    lane = _iota(gates.shape, 1)
    gc = jnp.where(lane < MLSTM_HEADS, gates, _log_sigmoid(gates))
    gcol_ref[...] = gc
    if maybe_grow:
        maybe_grow[0][...] = gc.T


def _hyb_in(x, g, w, wg, bias, tm, with_rows, seq_len=None):
    n, d = x.shape
    row = lambda c: pl.BlockSpec((tm, c), lambda i: (i, 0))
    outs = [(256, BF16), (256, BF16), (512, BF16), (512, F32), (512, BF16), (512, F32), (512, F32),
            (512, BF16), (512, BF16), (LANES, F32)]
    out_shape = [jax.ShapeDtypeStruct((n, c), t) for c, t in outs]
    out_specs = [row(c) for c, _ in outs]
    if with_rows:
        nt = seq_len // tm
        for i in (5, 6, 8):
            out_shape[i] = jax.ShapeDtypeStruct((n // seq_len, 512, seq_len), outs[i][1])
            out_specs[i] = pl.BlockSpec((None, 512, tm), lambda i: (i // nt, 0, i % nt))
        out_shape.append(jax.ShapeDtypeStruct((LANES, n), F32))
        out_specs.append(pl.BlockSpec((LANES, tm), lambda i: (0, i)))
    return pl.pallas_call(
        _hyb_in_kernel, grid=(n // tm,),
        in_specs=[row(d), _const_spec((1, d)), _const_spec(w.shape), _const_spec(wg.shape),
                  _const_spec((1, LANES))],
        out_specs=out_specs, out_shape=out_shape,
        compiler_params=_cparams(("parallel",)),
    )(x, g, w, wg, bias)


def _fox_prep_kernel(fq_ref, fk_ref, gcol_ref, selq_ref, selk_ref, cq_ref, ck_ref,
                     qa_ref, ka_ref, carry_ref):
    @pl.when(pl.program_id(1) == 0)
    def _():
        carry_ref[...] = jnp.zeros_like(carry_ref)

    tm = gcol_ref.shape[0]
    cum = _dot3_left(_tri(tm, "lower"), gcol_ref[...]) + carry_ref[...]
    carry_ref[...] = cum[tm - 1:tm, :]
    hi, mid, lo = _split3(cum * LOG2E)
    xq =jnp.concatenate([fq_ref[...], hi, mid, lo], axis=1)
    xk = jnp.concatenate([fk_ref[...], hi, mid, lo], axis=1)
    qa_ref[...] = (jnp.dot(xq, selq_ref[...], preferred_element_type=F32) + cq_ref[...]).astype(BF16)
    ka_ref[...] = (jnp.dot(xk, selk_ref[...], preferred_element_type=F32) + ck_ref[...]).astype(BF16)


def _fox_aug_constants():
    h = np.arange(FOX_HEADS)
    d = np.arange(FOX_DH)
    rows_qk = (h[:, None] * FOX_DH + d[None, :]).reshape(-1)
    cols_qk = (h[:, None] * LANES + d[None, :]).reshape(-1)
    base = 4 * LANES
    selq = np.zeros((4 * LANES + 3 * LANES, FOX_HEADS * LANES), np.float32)
    selk = np.zeros_like(selq)
    selq[rows_qk, cols_qk] = 1.0
    selk[rows_qk, cols_qk] = 1.0
    cq = np.zeros((1, FOX_HEADS * LANES), np.float32)
    ck = np.zeros_like(cq)
    for part in range(3):
        src = base + part * LANES + 2 * MLSTM_HEADS + h
        selq[src, h * LANES + FOX_DH + part] = 1.0
        selk[src, h * LANES + FOX_DH + 3 + part] = -1.0
        cq[0, h * LANES + FOX_DH + 3 + part] = 1.0
        ck[0, h * LANES + FOX_DH + part] = 1.0
    return jnp.asarray(selq, BF16), jnp.asarray(selk, BF16), jnp.asarray(cq), jnp.asarray(ck)


def _fox_prep(fq, fk16, gcol, b, t, tm):
    n = b * t
    nt = t // tm
    selq, selk, cq, ck = _fox_aug_constants()
    row = lambda c: pl.BlockSpec((tm, c), lambda bi, i: (bi * nt + i, 0))
    width = FOX_HEADS * LANES
    return pl.pallas_call(
        _fox_prep_kernel, grid=(b, nt),
        in_specs=[row(512), row(512), row(LANES), _const_spec(selq.shape), _const_spec(selk.shape),
                  _const_spec(cq.shape), _const_spec(ck.shape)],
        out_specs=[row(width), row(width)],
        out_shape=[jax.ShapeDtypeStruct((n, width), BF16)] * 2,
        scratch_shapes=[pltpu.VMEM((1, LANES), F32)],
        compiler_params=_cparams(("parallel", "arbitrary")),
    )(fq, fk16, gcol, selq, selk, cq, ck)


def _flash_kernel(q_ref, k_ref, vt_ref, o_ref, s_scr, p_scr, m_scr, l_scr, al_scr, acc_scr, *, blk):
    i = pl.program_id(2)
    keep = _iota((blk, blk), 0) <= _iota((blk, blk), 1)
    m_scr[...] = jnp.full(m_scr.shape, NEG, F32)
    l_scr[...] = jnp.zeros(l_scr.shape, F32)
    acc_scr[...] = jnp.zeros(acc_scr.shape, F32)

    def step(j, masked):
        off = pl.multiple_of(j * blk, blk)
        kb = k_ref[pl.ds(off, blk), :]
        vt = vt_ref[:, pl.ds(off, blk)]
        for hh in range(2):
            st = _dot_nt(kb[:, hh * LANES:(hh + 1) * LANES], q_ref[:, hh * LANES:(hh + 1) * LANES])
            s_scr[hh] = jnp.where(keep, st, NEG) if masked else st
        sub = LANES
        rows8 = lambda x: x.reshape(sub // 8, 8, LANES)
        for hh in range(2):
            for c in range(blk // LANES):
                cs = slice(c * LANES, (c + 1) * LANES)
                mx8 = jnp.full((8, LANES), NEG, F32)
                for kc in range(blk // sub):
                    mx8 = jnp.maximum(mx8, jnp.max(rows8(s_scr[hh, kc * sub:(kc + 1) * sub, cs]), axis=0))
                m_old = m_scr[hh, :, cs]
                mn = jnp.maximum(m_old, jnp.max(mx8, axis=0, keepdims=True))
                al = jnp.exp2(m_old - mn)
                m_scr[hh, :, cs] = mn
                al_scr[hh, :, cs] = al
                sum8 = jnp.zeros((8, LANES), F32)
                for kc in range(blk // sub):
                    ks = slice(kc * sub, (kc + 1) * sub)
                    p = jnp.exp2(s_scr[hh, ks, cs] - mn)
                    sum8 = sum8 + jnp.sum(rows8(p), axis=0)
                    p_scr[hh, ks, cs] = p.astype(BF16)
                l_scr[hh, :, cs] = al * l_scr[hh, :, cs] + jnp.sum(sum8, axis=0, keepdims=True)
            acc_scr[hh] = al_scr[hh] * acc_scr[hh] + jnp.dot(vt, p_scr[hh], preferred_element_type=F32)

    def body(j, carry):
        step(j, False)
        return carry

    lax.fori_loop(0, i, body, 0)
    step(i, True)
    row = _iota((LANES, blk), 0)
    o_ref[...] = jnp.where(row < FOX_DH, acc_scr[0] / l_scr[0], acc_scr[1] / l_scr[1]).T.astype(BF16)


def _fox_flash(q_aug, k_aug, vt16, b, t, blk):
    nq = t // blk
    pairs = FOX_HEADS // 2
    return pl.pallas_call(
        functools.partial(_flash_kernel, blk=blk), grid=(b, pairs, nq),
        in_specs=[pl.BlockSpec((blk, 2 * LANES), lambda bi, p, i: (bi * nq + i, p)),
                  pl.BlockSpec((t, 2 * LANES), lambda bi, p, i: (bi, p)),
                  pl.BlockSpec((None, LANES, t), lambda bi, p, i: (bi, p, 0))],
        out_specs=pl.BlockSpec((blk, LANES), lambda bi, p, i: (bi * nq + i, p)),
        out_shape=jax.ShapeDtypeStruct((b * t, FOX_HEADS * FOX_DH), BF16),
        scratch_shapes=[pltpu.VMEM((2, blk, blk), F32), pltpu.VMEM((2, blk, blk), BF16),
                        pltpu.VMEM((2, 1, blk), F32), pltpu.VMEM((2, 1, blk), F32), pltpu.VMEM((2, 1, blk), F32),
                        pltpu.VMEM((2, LANES, blk), F32)],
        compiler_params=_cparams(("parallel", "parallel", "arbitrary")),
    )(q_aug, k_aug, vt16)


def _mlstm_chunk_kernel(q_ref, k_ref, v_ref, o_ref, gcol_ref, grow_ref, outg_ref, c0_ref, n0_ref, m0_ref,
                        hm_ref, c_out, n_out, m_out, sc_ref, ms_ref):
    L = q_ref.shape[0]
    ci = pl.program_id(1)

    @pl.when(ci == 0)
    def _():
        sc_ref[:, :, 0:MLSTM_DV] = c0_ref[...]
        sc_ref[:, :, MLSTM_DV:] = n0_ref[...]
        ms_ref[...] = m0_ref[...]

    gcol = gcol_ref[...]
    grow = grow_ref[...]
    bcol_all = _dot3_left(_tri(L, "lower"), gcol)
    brow_all = _dot3_right(grow, _tri(L, "upper"))
    causal = _iota((L, L), 0) >= _iota((L, L), 1)
    ones = jnp.ones((L, MLSTM_DV), BF16)
    for h in range(MLSTM_HEADS):
        q = q_ref[:, h * MLSTM_DK:(h + 1) * MLSTM_DK]
        k = k_ref[:, h * MLSTM_DK:(h + 1) * MLSTM_DK]
        v1 = jnp.concatenate([v_ref[:, h * MLSTM_DV:(h + 1) * MLSTM_DV], ones], axis=1)
        igc, bc = gcol[:, h:h + 1], bcol_all[:, MLSTM_HEADS + h:MLSTM_HEADS + h + 1]
        igr, br = grow[h:h + 1, :], brow_all[MLSTM_HEADS + h:MLSTM_HEADS + h + 1, :]
        mp = ms_ref[h:h + 1, 0:1]
        d = jnp.where(causal, bc - br + igr, NEG)
        inter = bc + mp
        mt = jnp.maximum(inter, jnp.max(d, axis=1, keepdims=True))
        wi = jnp.exp(d - mt)
        we = jnp.exp(inter - mt)
        s = _dot_nt(q, k) * wi
        sc = sc_ref[h]
        qs = _dot(q, sc)
        sv = jnp.dot(s.astype(BF16), v1, preferred_element_type=F32)
        num = we * qs[:, 0:MLSTM_DV] + sv[:, 0:MLSTM_DV]
        den = we * qs[:, MLSTM_DV:MLSTM_DV + 1] + sv[:, MLSTM_DV:MLSTM_DV + 1]
        hh = num / jnp.maximum(jnp.abs(den), jnp.exp(-mt))
        hh = hh * lax.rsqrt(jnp.mean(hh * hh, axis=-1, keepdims=True) + EPS)
        sl = slice(h * MLSTM_DV, (h + 1) * MLSTM_DV)
        hm_ref[:, sl] = (hh * outg_ref[:, sl] * _sigmoid(o_ref[:, sl])).astype(BF16)
        bl = bc[L - 1:L, :]
        gr = bl - br + igr
        gcl = bl - bc + igc
        mn = jnp.maximum(bl + mp, jnp.max(gr, axis=1, keepdims=True))
        wk = jnp.exp(gcl - mn)
        dec = jnp.exp(bl + mp - mn)
        sc_ref[h] = dec * sc + _dot_tn(k.astype(F32) * wk, v1)
        ms_ref[h:h + 1, :] = jnp.broadcast_to(mn, (1, LANES))

    @pl.when(ci == pl.num_programs(1) - 1)
    def _():
        c_out[...] = sc_ref[:, :, 0:MLSTM_DV]
        n_out[...] = sc_ref[:, :, MLSTM_DV:]
        m_out[...] = ms_ref[...]


def _mlstm_chunk(mq, mk, mv, mo, gcol, grow, outg, c0, n0rep, m0rep, b, t):
    L = min(MLSTM_CHUNK, t)
    nc = t // L
    row = lambda c: pl.BlockSpec((L, c), lambda bi, ci: (bi * nc + ci, 0))
    st = lambda *s: pl.BlockSpec((None,) + s, lambda bi, ci: (bi,) + (0,) * len(s))
    H, DK, DV = MLSTM_HEADS, MLSTM_DK, MLSTM_DV
    return pl.pallas_call(
        _mlstm_chunk_kernel, grid=(b, nc),
        in_specs=[row(256), row(256), row(512), row(512), row(LANES),
                  pl.BlockSpec((8, L), lambda bi, ci: (0, bi * nc + ci)),
                  _const_spec((1, H * DV)), st(H, DK, DV), st(H, DK, DV), st(8, LANES)],
        out_specs=[row(H * DV), st(H, DK, DV), st(H, DK, DV), st(8, LANES)],
        out_shape=[jax.ShapeDtypeStruct((b * t, H * DV), BF16),
                   jax.ShapeDtypeStruct((b, H, DK, DV), F32),
                   jax.ShapeDtypeStruct((b, H, DK, DV), F32),
                   jax.ShapeDtypeStruct((b, 8, LANES), F32)],
        scratch_shapes=[pltpu.VMEM((H, DK, 2 * DV), F32), pltpu.VMEM((8, LANES), F32)],
        compiler_params=_cparams(("parallel", "arbitrary")),
    )(mq, mk, mv, mo, gcol, grow, outg, c0, n0rep, m0rep)


def _ffn_tail(x, mixed, g_ref, wgate_ref, wup_ref, wdown_ref, o_ref, n_chunks):
    x1 = x + _rms(mixed, g_ref[0:1, :])
    h = _rms(x1, g_ref[1:2, :]).astype(BF16)
    ch = wgate_ref.shape[1] // n_chunks
    y = jnp.zeros(x.shape, F32)
    for c in range(n_chunks):
        sl = slice(c * ch, (c + 1) * ch)
        a = jnp.dot(h, wgate_ref[:, sl], preferred_element_type=F32)
        u = jnp.dot(h, wup_ref[:, sl], preferred_element_type=F32)
        act = (a * _sigmoid(a) * u).astype(BF16)
        y = y + jnp.dot(act, wdown_ref[sl, :], preferred_element_type=F32)
    o_ref[...] = x1 + _rms(y, g_ref[2:3, :])


def _hyb_out_kernel(x_ref, hm_ref, hf_ref, wo_ref, g_ref, wgate_ref, wup_ref, wdown_ref, o_ref, *, n_chunks):
    half = hm_ref.shape[1]
    mixed = (jnp.dot(hm_ref[...], wo_ref[0:half, :], preferred_element_type=F32)
             + jnp.dot(hf_ref[...], wo_ref[half:, :], preferred_element_type=F32))
    _ffn_tail(x_ref[...], mixed, g_ref, wgate_ref, wup_ref, wdown_ref, o_ref, n_chunks)


def _seg_sum(x, bd):
    hi = x.astype(BF16)
    lo = (x - hi.astype(F32)).astype(BF16)
    parts = []
    for p in range(x.shape[1] // LANES):
        sl = slice(p * LANES, (p + 1) * LANES)
        parts.append(jnp.dot(hi[:, sl], bd, preferred_element_type=F32)
                     + jnp.dot(lo[:, sl], bd, preferred_element_type=F32))
    return jnp.concatenate(parts, axis=1)


def _block_diag_ones():
    r, c = _iota((LANES, LANES), 0), _iota((LANES, LANES), 1)
    return jnp.where((r // RWKV_DH) == (c // RWKV_DH), 1.0, 0.0).astype(BF16)


def _rwkv_out_kernel(x_ref, y_ref, r_ref, k_ref, v_ref, gate_ref, vec_ref, wo_ref, g_ref,
                     wgate_ref, wup_ref, wdown_ref, o_ref, *, n_chunks, y_transposed):
    bd = _block_diag_ones()
    y = y_ref[...].T if y_transposed else y_ref[...]
    inv = 1.0 / RWKV_DH
    mu = _seg_sum(y, bd) * inv
    yc = y - mu
    var = _seg_sum(yc * yc, bd) * inv
    yn = yc * lax.rsqrt(var + GN_EPS) * vec_ref[0:1, :] + vec_ref[1:2, :]
    r = r_ref[...].astype(F32)
    bonus = _seg_sum(r * k_ref[...].astype(F32) * vec_ref[2:3, :], bd) * v_ref[...].astype(F32)
    z = ((yn + bonus) * gate_ref[...].astype(F32)).astype(BF16)
    mixed = jnp.dot(z, wo_ref[...], preferred_element_type=F32)
    _ffn_tail(x_ref[...], mixed, g_ref, wgate_ref, wup_ref, wdown_ref, o_ref, n_chunks)


def _ffn_chunks(hidden):
    for n in (2, 4, 1):
        if hidden % (n * LANES) == 0:
            return n
    return 1


def _hyb_out(x, hm, hf, wo, g3, wgate, wup, wdown, tm):
    n, d = x.shape
    row = lambda c: pl.BlockSpec((tm, c), lambda i: (i, 0))
    return pl.pallas_call(
        functools.partial(_hyb_out_kernel, n_chunks=_ffn_chunks(wgate.shape[1])), grid=(n // tm,),
        in_specs=[row(d), row(hm.shape[1]), row(hf.shape[1]), _const_spec(wo.shape), _const_spec(g3.shape),
                  _const_spec(wgate.shape), _const_spec(wup.shape), _const_spec(wdown.shape)],
        out_specs=row(d), out_shape=jax.ShapeDtypeStruct((n, d), F32),
        compiler_params=_cparams(("parallel",)),
    )(x, hm, hf, wo, g3, wgate, wup, wdown)


def _rwkv_out(x, y, r, k, v, gate, vec, wo, g3, wgate, wup, wdown, tm, y_transposed=False):
    n, d = x.shape
    assert not y_transposed or tm == n
    row = lambda c: pl.BlockSpec((tm, c), lambda i: (i, 0))
    y_spec = pl.BlockSpec((d, n), lambda i: (0, 0)) if y_transposed else row(d)
    return pl.pallas_call(
        functools.partial(_rwkv_out_kernel, n_chunks=_ffn_chunks(wgate.shape[1]), y_transposed=y_transposed),
        grid=(n // tm,),
        in_specs=[row(d), y_spec] + [row(d)] * 4 + [_const_spec(vec.shape), _const_spec(wo.shape),
                                                    _const_spec(g3.shape), _const_spec(wgate.shape),
                                                    _const_spec(wup.shape), _const_spec(wdown.shape)],
        out_specs=row(d), out_shape=jax.ShapeDtypeStruct((n, d), F32),
        compiler_params=_cparams(("parallel",)),
    )(x, y, r, k, v, gate, vec, wo, g3, wgate, wup, wdown)


def _rwkv_proj_body(h, hprev, vec_ref, wr_ref, wk_ref, wv_ref, w1_ref, w2_ref, a1_ref, a2_ref, g1_ref, g2_ref,
                    r_ref, k_ref, v_ref, na_ref, bb_ref, lw_ref, gate_ref):
    xx = hprev - h
    mixed = lambda i: (h + xx * vec_ref[i:i + 1, :]).astype(BF16)
    r = jnp.dot(mixed(0), wr_ref[...], preferred_element_type=F32)
    k = jnp.dot(mixed(2), wk_ref[...], preferred_element_type=F32)
    v = jnp.dot(mixed(3), wv_ref[...], preferred_element_type=F32)
    wl = vec_ref[6:7, :] + _dot(jnp.tanh(jnp.dot(mixed(1), w1_ref[...], preferred_element_type=F32)), w2_ref[...])
    w_log = -_softplus(-wl) - 0.5
    a = _sigmoid(vec_ref[7:8, :] + _dot(jnp.dot(mixed(4), a1_ref[...], preferred_element_type=F32), a2_ref[...]))
    gate = _dot(_sigmoid(jnp.dot(mixed(5), g1_ref[...], preferred_element_type=F32)), g2_ref[...])
    kk = k * vec_ref[8:9, :]
    nrm = jnp.sqrt(_seg_sum(kk * kk, _block_diag_ones()))
    kk = kk / jnp.maximum(nrm, 1e-12)
    k2 = k * (1.0 + (a - 1.0) * vec_ref[9:10, :])
    lw = -jnp.exp(w_log)
    r_ref[...] = r.astype(BF16)
    k_ref[...] = k2.astype(BF16)
    v_ref[...] = v.astype(BF16)
    na_ref[...] = (-kk).astype(BF16)
    bb_ref[...] = (kk * a).astype(BF16)
    lw_ref[...] = lw
    gate_ref[...] = gate.astype(BF16)
    return r, k2, v, -kk, kk * a, lw


def _rwkv_proj_seq_kernel(x_ref, xprev_ref, shift_ref, g_ref, vec_ref, *rest):
    weights, outs = rest[:9], rest[9:]
    hlast_ref = outs[-1]
    g = g_ref[...]
    h = _rms(x_ref[...], g)
    tm = h.shape[0]
    prev_tile_last = _rms(xprev_ref[...], g)[7:8, :]
    first = jnp.where(pl.program_id(1) == 0, shift_ref[...], prev_tile_last)
    hprev = jnp.where(_iota(h.shape, 0) == 0, first, pltpu.roll(h, 1, 0))
    _rwkv_proj_body(h, hprev, vec_ref, *weights, *outs[:-1])
    hlast_ref[...] = h[tm - 8:tm, :]


def _rwkv_proj_step_kernel(x_ref, shift_ref, g_ref, vec_ref, *rest):
    weights, outs, t_outs = rest[:9], rest[9:17], rest[17:]
    h = _rms(x_ref[...], g_ref[...])
    vals = _rwkv_proj_body(h, shift_ref[...], vec_ref, *weights, *outs[:-1])
    outs[-1][...] = h
    for val, ref in zip(vals, t_outs):
        ref[...] = val.T


_RWKV_PROJ_OUT = (BF16, BF16, BF16, BF16, BF16, F32, BF16)


def _rwkv_proj_seq(x, shift0, g, vec, weights, b, t, tm):
    n, d = x.shape
    nt = t // tm
    row = pl.BlockSpec((tm, d), lambda bi, i: (bi * nt + i, 0))
    prev = pl.BlockSpec((8, d), lambda bi, i: (jnp.maximum((bi * nt + i) * (tm // 8) - 1, 0), 0))
    return pl.pallas_call(
        _rwkv_proj_seq_kernel, grid=(b, nt),
        in_specs=[row, prev, pl.BlockSpec((None, 1, d), lambda bi, i: (bi, 0, 0)), _const_spec((1, d)),
                  _const_spec(vec.shape)] + [_const_spec(w.shape) for w in weights],
        out_specs=[row] * 7 + [pl.BlockSpec((None, 8, d), lambda bi, i: (bi, 0, 0))],
        out_shape=[jax.ShapeDtypeStruct((n, d), dt) for dt in _RWKV_PROJ_OUT]
        + [jax.ShapeDtypeStruct((b, 8, d), F32)],
        compiler_params=_cparams(("parallel", "arbitrary")),
    )(x, x, shift0.reshape(b, 1, d), g, vec, *weights)


def _rwkv_proj_step(x, shift0, g, vec, weights):
    n, d = x.shape
    full = pl.BlockSpec((n, d), lambda i: (0, 0))
    return pl.pallas_call(
        _rwkv_proj_step_kernel, grid=(1,),
        in_specs=[full, full, _const_spec((1, d)), _const_spec(vec.shape)] + [_const_spec(w.shape) for w in weights],
        out_specs=[full] * 8 + [pl.BlockSpec((d, n), lambda i: (0, 0))] * 6,
        out_shape=[jax.ShapeDtypeStruct((n, d), dt) for dt in _RWKV_PROJ_OUT + (F32,)]
        + [jax.ShapeDtypeStruct((d, n), F32)] * 6,
        compiler_params=_cparams(("arbitrary",)),
    )(x, shift0, g, vec, *weights)


def _rwkv_chunk_kernel(r_ref, k_ref, v_ref, na_ref, bb_ref, lw_ref, s0_ref, y_ref, s_out, st_ref):
    nb, L, d = r_ref.shape
    pairs = d // LANES
    items = [(b, p) for b in range(nb) for p in range(pairs)]
    ci = pl.program_id(0)

    @pl.when(ci == 0)
    def _():
        st_ref[...] = s0_ref[...]

    n2 = 2 * L
    rr, cc = _iota((n2, n2), 0), _iota((n2, n2), 1)
    strict = rr > cc
    incl = rr >= cc
    m0 = _iota((L, LANES), 1) < RWKV_DH
    stack = lambda x: jnp.concatenate([jnp.where(m0, x, 0.0), jnp.where(m0, 0.0, x)], axis=0)
    tril = _tri(L, "lower")
    cw = [_dot3_left(tril, lw_ref[b]) for b in range(nb)]
    xs, zs, ze, vs, decay = [], [], [], [], []
    for b, p in items:
        sl = slice(p * LANES, (p + 1) * LANES)
        cwp, lwp = cw[b][:, sl], lw_ref[b, :, sl]
        wl = cwp[L - 1:L, :]
        w_t = jnp.exp(cwp)
        w_inv = jnp.exp(-cwp)
        w_prev = jnp.exp(cwp - lwp)
        w_end = jnp.exp(wl - cwp)
        r, k = r_ref[b, :, sl].astype(F32), k_ref[b, :, sl].astype(F32)
        na, bb = na_ref[b, :, sl].astype(F32), bb_ref[b, :, sl].astype(F32)
        xs.append(jnp.concatenate([stack(na * w_prev), stack(r * w_t)], axis=0).astype(BF16))
        zs.append(jnp.concatenate([stack(bb * w_inv), stack(k * w_inv)], axis=0).astype(BF16))
        ze.append(jnp.concatenate([stack(bb * w_end), stack(k * w_end)], axis=0).astype(BF16))
        vs.append(stack(v_ref[b, :, sl].astype(F32)).astype(BF16))
        decay.append(jnp.exp(wl))
    every = range(len(items))
    st = [st_ref[i] for i in every]
    g = [_dot_nt(xs[i], zs[i]) for i in every]
    xst = [_dot_nt(xs[i], st[i]) for i in every]
    nmat = [jnp.where(strict, g[i][0:n2, 0:n2], 0.0) for i in every]
    a_cat = [jnp.concatenate([jnp.where(strict, g[i][0:n2, n2:], 0.0), jnp.where(incl, g[i][n2:, n2:], 0.0)],
                             axis=0).astype(BF16) for i in every]
    a_rb = [jnp.where(incl, g[i][n2:, 0:n2], 0.0).astype(BF16) for i in every]
    av = [jnp.dot(a_cat[i], vs[i], preferred_element_type=F32) for i in every]
    eye = jnp.where(rr == cc, 1.0, 0.0)
    t = [eye + jnp.where(rr // 2 == cc // 2, nmat[i], 0.0) for i in every]
    bsz = 2
    while bsz < L:
        sel = (rr // (2 * bsz) == cc // (2 * bsz)) & (rr // bsz != cc // bsz)
        tb = [t[i].astype(BF16) for i in every]
        half = [jnp.dot(jnp.where(sel, nmat[i], 0.0).astype(BF16), tb[i], preferred_element_type=F32).astype(BF16)
                for i in every]
        t = [t[i] + jnp.dot(tb[i], half[i], preferred_element_type=F32) for i in every]
        bsz *= 2
    u = [_dot(t[i], xst[i][0:n2] + av[i][0:n2]).astype(BF16) for i in every]
    ys = [xst[i][n2:] + jnp.dot(a_rb[i], u[i], preferred_element_type=F32) + av[i][n2:] for i in every]
    snew = [_dot_tn(jnp.concatenate([u[i], vs[i]], axis=0), ze[i]) for i in every]
    for i, (b, p) in enumerate(items):
        y_ref[b, :, p * LANES:(p + 1) * LANES] = ys[i][0:L] + ys[i][L:]
        st_ref[i] = st[i] * decay[i] + snew[i]

    @pl.when(ci == pl.num_programs(0) - 1)
    def _():
        s_out[...] = st_ref[...]


def _rwkv_chunk(r, k, v, na, bb, lw, s0bd, b, t):
    L = min(RWKV_CHUNK, t)
    d = r.shape[1]
    pairs = d // LANES
    row = pl.BlockSpec((b, L, d), lambda ci: (0, ci, 0))
    st = pl.BlockSpec((b * pairs, LANES, LANES), lambda ci: (0, 0, 0))
    y, s = pl.pallas_call(
        _rwkv_chunk_kernel, grid=(t // L,),
        in_specs=[row] * 6 + [st], out_specs=[row, st],
        out_shape=[jax.ShapeDtypeStruct((b, t, d), F32), jax.ShapeDtypeStruct((b * pairs, LANES, LANES), F32)],
        scratch_shapes=[pltpu.VMEM((b * pairs, LANES, LANES), F32)],
        compiler_params=_cparams(("arbitrary",)),
    )(*(a.reshape(b, t, d) for a in (r, k, v, na, bb, lw)), s0bd.reshape(b * pairs, LANES, LANES))
    return y.reshape(b * t, d), s.reshape(b, pairs, LANES, LANES)


def _fox_decode_kernel(pt_ref, q_ref, knew_ref, vnew_ref, lfnew_ref, *rest, n_pages):
    del pt_ref
    k_refs, v_refs, lf_refs = rest[0:n_pages], rest[n_pages:2 * n_pages], rest[2 * n_pages:3 * n_pages]
    o_ref = rest[3 * n_pages]
    width = FOX_HEADS * FOX_DH
    hmask = (_iota((FOX_HEADS, width), 1) // FOX_DH) == _iota((FOX_HEADS, width), 0)
    qbd = jnp.where(hmask, jnp.broadcast_to(q_ref[...].astype(F32), (FOX_HEADS, width)), 0.0).astype(BF16)
    page = k_refs[0].shape[1]
    later = _tri(page, "strict_upper_t")
    run = lfnew_ref[...]
    logits = [None] * n_pages
    for j in reversed(range(n_pages)):
        lf = lf_refs[j][...]
        logits[j] = _dot(qbd, k_refs[j][...]) + _dot3_right(lf, later) + run
        run = run + jnp.sum(lf, axis=1, keepdims=True)
    s_self = _dot_nt(qbd, jnp.broadcast_to(knew_ref[...], (8, width)))[:, 0:1]
    m = s_self
    for j in range(n_pages):
        m = jnp.maximum(m, jnp.max(logits[j], axis=1, keepdims=True))
    p_self = jnp.exp(s_self - m)
    l = p_self
    acc = p_self * vnew_ref[...]
    for j in range(n_pages):
        p = jnp.exp(logits[j] - m)
        l = l + jnp.sum(p, axis=1, keepdims=True)
        acc = acc + _dot_nt(p, v_refs[j][...])
    o_ref[...] = jnp.sum(jnp.where(hmask, acc / l, 0.0), axis=0, keepdims=True)


def _fox_decode(page_table, q, knew, vnew, lfnew, cache_kt, cache_vt, cache_lf_t):
    nb, n_pages = page_table.shape
    width = q.shape[1]
    page = cache_kt.shape[2]
    row = pl.BlockSpec((None, 1, width), lambda b, pt: (b, 0, 0))
    paged = lambda j, shape: pl.BlockSpec((None,) + shape, lambda b, pt, j=j: (pt[b * n_pages + j], 0, 0))
    in_specs = ([row, row, row, pl.BlockSpec((None, FOX_HEADS, 1), lambda b, pt: (b, 0, 0))]
                + [paged(j, (width, page)) for j in range(n_pages)]
                + [paged(j, (width, page)) for j in range(n_pages)]
                + [paged(j, (FOX_HEADS, page)) for j in range(n_pages)])
    out = pl.pallas_call(
        functools.partial(_fox_decode_kernel, n_pages=n_pages),
        grid_spec=pltpu.PrefetchScalarGridSpec(
            num_scalar_prefetch=1, grid=(nb,), in_specs=in_specs,
            out_specs=pl.BlockSpec((None, 1, width), lambda b, pt: (b, 0, 0))),
        out_shape=jax.ShapeDtypeStruct((nb, 1, width), F32),
        compiler_params=_cparams(("arbitrary",)),
    )(page_table.reshape(-1), q.reshape(nb, 1, width), knew.reshape(nb, 1, width), vnew.reshape(nb, 1, width),
      lfnew.reshape(nb, FOX_HEADS, 1), *([cache_kt] * n_pages), *([cache_vt] * n_pages), *([cache_lf_t] * n_pages))
    return out.reshape(nb, width)


def _columns(rows):
    n = rows.shape[0]
    hi, mid, lo = _split3(rows)
    r, c = _iota((RWKV_DH, RWKV_DH), 0), _iota((RWKV_DH, RWKV_DH), 1)
    eye = jnp.where(r == c, 1.0, 0.0).astype(BF16)
    t = lax.dot_general(eye, jnp.concatenate([hi, mid, lo], axis=0), (((1,), (1,)), ((), ())),
                        preferred_element_type=F32)
    return (t[:, 2 * n:3 * n] + t[:, n:2 * n]) + t[:, 0:n]


def _mlstm_step_kernel(q_ref, k_ref, v_ref, o_ref, gs_ref, m0_ref, outg_ref, c0_ref, n0_ref,
                       hm_ref, c_out, n_out, m_out):
    kf = k_ref[...].astype(F32)
    kcols = _columns(jnp.concatenate([kf, jnp.zeros((8 - MLSTM_HEADS, MLSTM_DK), F32)], axis=0))
    gs = gs_ref[...]
    m0 = m0_ref[...]
    for h in range(MLSTM_HEADS):
        q, k, v = q_ref[h:h + 1, :].astype(F32), kf[h:h + 1, :], v_ref[h:h + 1, :].astype(F32)
        ig, lf, mp = gs[:, h:h + 1], gs[:, MLSTM_HEADS + h:MLSTM_HEADS + h + 1], m0[:, h:h + 1]
        inter = lf + mp
        mt = jnp.maximum(inter, ig)
        wi = jnp.exp(ig - mt)
        we = jnp.exp(inter - mt)
        s = jnp.sum(q * k, axis=1, keepdims=True) * wi
        c = c0_ref[h]
        n = n0_ref[h:h + 1, :]
        qc = _dot(jnp.broadcast_to(q, (8, MLSTM_DK)), c)[0:1, :]
        num = we * qc + s * v
        den = we * jnp.sum(q * n, axis=1, keepdims=True) + s
        hh = num / jnp.maximum(jnp.abs(den), jnp.exp(-mt))
        hh = hh * lax.rsqrt(jnp.mean(hh * hh, axis=-1, keepdims=True) + EPS)
        sl = slice(h * MLSTM_DV, (h + 1) * MLSTM_DV)
        hm_ref[:, sl] = hh * outg_ref[:, sl] * _sigmoid(o_ref[:, sl])
        c_out[h] = we * c + (wi * kcols[:, h:h + 1]) * v
        n_out[h:h + 1, :] = we * n + wi * k
        m_out[:, h:h + 1] = mt


def _mlstm_step(mq, mk, mv, mo, gcol, outg, c0, n0, m0):
    nb = mq.shape[0]
    H, DK, DV = MLSTM_HEADS, MLSTM_DK, MLSTM_DV
    per = lambda *s: pl.BlockSpec((None,) + s, lambda b: (b,) + (0,) * len(s))
    hm, c, n, m = pl.pallas_call(
        _mlstm_step_kernel, grid=(nb,),
        in_specs=[per(H, DK), per(H, DK), per(H, DV), per(1, H * DV), per(1, LANES), per(1, H),
                  _const_spec((1, H * DV)), per(H, DK, DV), per(H, DK)],
        out_specs=[per(1, H * DV), per(H, DK, DV), per(H, DK), per(1, H)],
        out_shape=[jax.ShapeDtypeStruct((nb, 1, H * DV), F32), jax.ShapeDtypeStruct((nb, H, DK, DV), F32),
                   jax.ShapeDtypeStruct((nb, H, DK), F32), jax.ShapeDtypeStruct((nb, 1, H), F32)],
        compiler_params=_cparams(("arbitrary",)),
    )(mq.reshape(nb, H, DK), mk.reshape(nb, H, DK), mv.reshape(nb, H, DV), mo.reshape(nb, 1, H * DV),
      gcol.reshape(nb, 1, LANES), m0.reshape(nb, 1, H), outg, c0, n0)
    return hm.reshape(nb, H * DV), c, n, m.reshape(nb, H)


def _rwkv_step_kernel(r_ref, k_ref, v_ref, na_ref, bb_ref, lw_ref, s0_ref, y_ref, s_out):
    w, a, bv, kv, rv = jnp.exp(lw_ref[...]), na_ref[...], bb_ref[...], k_ref[...], r_ref[...]

    def group(g8, carry):
        base = pl.multiple_of(g8 * 8, 8)
        vrows = v_ref[pl.ds(base, 8), :]
        ys = []
        for i in range(8):
            s = s0_ref[base + i]
            sa = jnp.sum(s * a, axis=0, keepdims=True)
            sn = s * w + sa * bv + vrows[i:i + 1, :] * kv
            s_out[base + i] = sn
            ys.append(jnp.sum(sn * rv, axis=0, keepdims=True))
        y_ref[pl.ds(base, 8), :] = jnp.concatenate(ys, axis=0)
        return carry

    lax.fori_loop(0, RWKV_DH // 8, group, 0)


def _rwkv_step(r_t, k_t, v_t, na_t, bb_t, lw_t, s0_t):
    d, nb = r_t.shape
    H, N = RWKV_HEADS, RWKV_DH
    vec = pl.BlockSpec((N, nb), lambda h: (h, 0))
    st = pl.BlockSpec((None, N, N, nb), lambda h: (h, 0, 0, 0))
    return pl.pallas_call(
        _rwkv_step_kernel, grid=(H,),
        in_specs=[vec] * 6 + [st], out_specs=[vec, st],
        out_shape=[jax.ShapeDtypeStruct((d, nb), F32), jax.ShapeDtypeStruct((H, N, N, nb), F32)],
        compiler_params=_cparams(("parallel",)),
    )(r_t, k_t, v_t, na_t, bb_t, lw_t, s0_t)


def _pair_block_diag_inv(sbd, n):
    return jnp.stack([sbd[:, :, 0:n, 0:n], sbd[:, :, n:, n:]], axis=2).reshape(sbd.shape[0], -1, n, n)


def kernel(x_prompt, x_sample, cache_fox_k, cache_fox_v, cache_fox_lf, state_mlstm_c, state_mlstm_n, state_mlstm_m, state_rwkv_s, state_rwkv_shift, page_table, norm_g, hyb_w_in, mlstm_i_bias, mlstm_f_bias, mlstm_out_g, fox_f_bias, hyb_w_out, rwkv_mix, rwkv_w_r, rwkv_w_k, rwkv_w_v, rwkv_w_o, rwkv_w0, rwkv_w1, rwkv_w2, rwkv_a0, rwkv_a1, rwkv_a2, rwkv_g1, rwkv_g2, rwkv_k_k, rwkv_k_a, rwkv_r_k, rwkv_ln_w, rwkv_ln_b, ffn_w_gate, ffn_w_up, ffn_w_down):
    bp, t, d = x_prompt.shape
    bs = x_sample.shape[0]
    assert x_sample.shape[1] == 1 and norm_g.shape[0] == 2
    n_phys, page = cache_fox_k.shape[1], cache_fox_k.shape[2]
    H, DK, DV = MLSTM_HEADS, MLSTM_DK, MLSTM_DV
    b16 = lambda a: a.astype(BF16)

    w_in = hyb_w_in[0]
    g_lo = 2 * H * DK + 2 * H * DV
    w_main = b16(jnp.concatenate([w_in[:, :g_lo], w_in[:, g_lo + 2 * H:g_lo + 2 * H + 3 * FOX_HEADS * FOX_DH]], axis=1))
    w_gate_cols = jnp.concatenate([w_in[:, g_lo:g_lo + 2 * H], w_in[:, -FOX_HEADS:]], axis=1)
    n_g = w_gate_cols.shape[1]
    w_g = b16(jnp.pad(w_gate_cols, ((0, 0), (0, LANES - n_g))))
    gate_bias = jnp.pad(jnp.concatenate([mlstm_i_bias[0], mlstm_f_bias[0], fox_f_bias[0]]), (0, LANES - n_g))[None, :]
    outg = mlstm_out_g[0][None, :]
    w_out = b16(hyb_w_out[0])
    ffn = [(b16(ffn_w_gate[l]), b16(ffn_w_up[l]), b16(ffn_w_down[l])) for l in range(2)]
    rw = [b16(a[0]) for a in (rwkv_w_r, rwkv_w_k, rwkv_w_v, rwkv_w1, rwkv_w2, rwkv_a1, rwkv_a2, rwkv_g1, rwkv_g2)]
    proj_vec = jnp.concatenate([rwkv_mix[0], rwkv_w0, rwkv_a0, rwkv_k_k, rwkv_k_a,
                                jnp.zeros((6, d), F32)], axis=0)
    out_vec = jnp.concatenate([rwkv_ln_w, rwkv_ln_b, rwkv_r_k.reshape(1, d), jnp.zeros((5, d), F32)], axis=0)
    w_o = b16(rwkv_w_o[0])

    def even_in(x, tm, with_rows, seq_len=None):
        return _hyb_in(x, norm_g[0, 0][None, :], w_main, w_g, gate_bias, tm, with_rows, seq_len)

    n = bp * t
    tm = min(ROW_TILE, t)
    xp = x_prompt.reshape(n, d)
    mq, mk, mv, mo, fq, fk_t, fv_t, fk16, fvt16, gcol, grow = even_in(xp, tm, True, t)
    q_aug, k_aug = _fox_prep(fq, fk16, gcol, bp, t, tm)
    hf = _fox_flash(q_aug, k_aug, fvt16, bp, t, min(ATTN_BLOCK, t))
    zc = jnp.zeros((bp, H, DK, DV), F32)
    hm, pc, pn_rep, pm_rep = _mlstm_chunk(mq, mk, mv, mo, gcol, grow, outg, zc, zc,
                                          jnp.zeros((bp, 8, LANES), F32), bp, t)
    x1 = _hyb_out(xp, hm, hf, w_out, norm_g[0, 1:4], *ffn[0], tm)
    r, k2, v, na, bb, lw, gate, hlast = _rwkv_proj_seq(x1, jnp.zeros((bp, d), F32), norm_g[1, 0][None, :],
                                                       proj_vec, rw, bp, t, tm)
    s0p = jnp.zeros((bp, RWKV_HEADS // 2, LANES, LANES), F32)
    y, ps_bd = _rwkv_chunk(r, k2, v, na, bb, lw, s0p, bp, t)
    y_prompt = _rwkv_out(x1, y, r, k2, v, gate, out_vec, w_o, norm_g[1, 1:4], *ffn[1], tm).reshape(bp, t, d)
    p_rwkv_s = _pair_block_diag_inv(ps_bd, RWKV_DH)

    xs = x_sample.reshape(bs, d)
    smq, smk, smv, smo, sfq, sfk, sfv, _, _, sgcol = even_in(xs, bs, False)
    lff_new = sgcol[:, 2 * H:2 * H + FOX_HEADS]
    shf = _fox_decode(page_table, sfq, sfk, sfv, lff_new,
                      jnp.transpose(cache_fox_k[0], (0, 2, 3, 1)).reshape(n_phys, -1, page),
                      jnp.transpose(cache_fox_v[0], (0, 2, 3, 1)).reshape(n_phys, -1, page),
                      jnp.swapaxes(cache_fox_lf[0], 1, 2))
    shm, sc, sn, sm = _mlstm_step(smq, smk, smv, smo, sgcol, outg, state_mlstm_c[0], state_mlstm_n[0],
                                  state_mlstm_m[0])
    xs1 = _hyb_out(xs, b16(shm), b16(shf), w_out, norm_g[0, 1:4], *ffn[0], bs)
    sr, sk2, sv, _, _, _, sgate, sh, *step_in = _rwkv_proj_step(xs1, state_rwkv_shift[0], norm_g[1, 0][None, :],
                                                                proj_vec, rw)
    sy_t, ss_t = _rwkv_step(*step_in, jnp.transpose(state_rwkv_s[0], (1, 2, 3, 0)))
    ss = jnp.transpose(ss_t, (3, 0, 1, 2))
    y_sample = _rwkv_out(xs1, sy_t, sr, sk2, sv, sgate, out_vec, w_o, norm_g[1, 1:4], *ffn[1], bs,
                         y_transposed=True).reshape(bs, 1, d)

    e = lambda a: a[None]
    return (y_prompt, y_sample,
            e(jnp.transpose(fk_t.reshape(bp, FOX_HEADS, FOX_DH, t), (0, 3, 1, 2))),
            e(jnp.transpose(fv_t.reshape(bp, FOX_HEADS, FOX_DH, t), (0, 3, 1, 2))),
            e(jnp.transpose(grow[2 * H:2 * H + FOX_HEADS].reshape(FOX_HEADS, bp, t), (1, 2, 0))),
            e(pc), e(pn_rep[..., 0]), e(pm_rep[:, :H, 0]), e(p_rwkv_s), e(hlast[:, 7, :]),
            e(sfk.reshape(bs, 1, FOX_HEADS, FOX_DH)), e(sfv.reshape(bs, 1, FOX_HEADS, FOX_DH)),
            e(lff_new.reshape(bs, 1, FOX_HEADS)),
            e(sc), e(sn), e(sm), e(ss), e(sh))
```

```python
import functools

import jax
import jax.numpy as jnp
import numpy as np
from jax import lax
from jax.experimental import pallas as pl
from jax.experimental.pallas import tpu as pltpu

F32, BF16 = jnp.float32, jnp.bfloat16
EPS = 1e-6
GN_EPS = 64e-5
NEG = -1e30
LOG2E = 1.4426950408889634

MLSTM_HEADS, MLSTM_DK, MLSTM_DV = 4, 64, 128
FOX_HEADS, FOX_DH = 8, 64
RWKV_HEADS, RWKV_DH = 16, 64
MLSTM_CHUNK = 128
RWKV_CHUNK = 64
LANES = 128
ATTN_BLOCK = 512
FLASH_HEADS = 4
ROW_TILE = 512
VMEM_LIMIT = 56 * 1024 * 1024


def _cparams(sem):
    return pltpu.CompilerParams(dimension_semantics=sem, vmem_limit_bytes=VMEM_LIMIT)


def _const_spec(shape):
    nd = len(shape)
    return pl.BlockSpec(shape, lambda *a: (0,) * nd, pipeline_mode=pl.Buffered(1))


def _dot(a, b):
    return jnp.dot(a.astype(BF16), b.astype(BF16), preferred_element_type=F32)


def _dot_nt(a, b):
    return lax.dot_general(a.astype(BF16), b.astype(BF16), (((1,), (1,)), ((), ())),
                           preferred_element_type=F32)


def _dot_tn(a, b):
    return lax.dot_general(a.astype(BF16), b.astype(BF16), (((0,), (0,)), ((), ())),
                           preferred_element_type=F32)


def _split3(x):
    hi = x.astype(BF16)
    r1 = x - hi.astype(F32)
    mid = r1.astype(BF16)
    lo = (r1 - mid.astype(F32)).astype(BF16)
    return hi, mid, lo


def _dot3_left(m, x):
    hi, mid, lo = _split3(x)
    d = lambda t: jnp.dot(m, t, preferred_element_type=F32)
    return (d(lo) + d(mid)) + d(hi)


def _dot3_right(x, m):
    hi, mid, lo = _split3(x)
    d = lambda t: jnp.dot(t, m, preferred_element_type=F32)
    return (d(lo) + d(mid)) + d(hi)


def _rms(x, g):
    return x * lax.rsqrt(jnp.mean(x * x, axis=-1, keepdims=True) + EPS) * g


def _sigmoid(x):
    return 1.0 / (1.0 + jnp.exp(-x))


def _log_sigmoid(x):
    return jnp.minimum(x, 0.0) - jnp.log1p(jnp.exp(-jnp.abs(x)))


def _softplus(x):
    return jnp.maximum(x, 0.0) + jnp.log1p(jnp.exp(-jnp.abs(x)))


def _iota(shape, dim):
    return lax.broadcasted_iota(jnp.int32, shape, dim)


def _tri(n, kind):
    r, c = _iota((n, n), 0), _iota((n, n), 1)
    m = {"lower": r >= c, "upper": r <= c, "strict_upper_t": r > c}[kind]
    return jnp.where(m, 1.0, 0.0).astype(BF16)


def _hyb_in_kernel(x_ref, g_ref, w_ref, wg_ref, bias_ref,
                   mq_ref, mk_ref, mv_ref, mo_ref, fq_ref, fk_ref, fv_ref, fk16_ref, fv16_ref,
                   gcol_ref, *maybe_grow):
    hn = _rms(x_ref[...], g_ref[...]).astype(BF16)
    seg = lambda lo, hi: jnp.dot(hn, w_ref[:, lo:hi], preferred_element_type=F32)
    mq_ref[...] = seg(0, 256).astype(BF16)
    mk_ref[...] = (seg(256, 512) * (MLSTM_DK ** -0.5)).astype(BF16)
    mv_ref[...] = seg(512, 1024).astype(BF16)
    mo_ref[...] = seg(1024, 1536)
    q_scale = FOX_DH ** -0.5 * (LOG2E if maybe_grow else 1.0)
    fq_ref[...] = (seg(1536, 2048) * q_scale).astype(BF16)
    fk = seg(2048, 2560)
    fv = seg(2560, 3072)
    fk_ref[...] = fk.T if maybe_grow else fk
    fv_ref[...] = fv.T if maybe_grow else fv
    fk16_ref[...] = fk.astype(BF16)
    fv16_ref[...] = (fv.T if maybe_grow else fv).astype(BF16)
    gates = jnp.dot(hn, wg_ref[...], preferred_element_type=F32) + bias_ref[...]
    lane = _iota(gates.shape, 1)
    gc = jnp.where(lane < MLSTM_HEADS, gates, _log_sigmoid(gates))
    gcol_ref[...] = gc
    if maybe_grow:
        maybe_grow[0][...] = gc.T


def _hyb_in(x, g, w, wg, bias, tm, with_rows, seq_len=None):
    n, d = x.shape
    row = lambda c: pl.BlockSpec((tm, c), lambda i: (i, 0))
    outs = [(256, BF16), (256, BF16), (512, BF16), (512, F32), (512, BF16), (512, F32), (512, F32),
            (512, BF16), (512, BF16), (LANES, F32)]
    out_shape = [jax.ShapeDtypeStruct((n, c), t) for c, t in outs]
    out_specs = [row(c) for c, _ in outs]
    if with_rows:
        nt = seq_len // tm
        for i in (5, 6, 8):
            out_shape[i] = jax.ShapeDtypeStruct((n // seq_len, 512, seq_len), outs[i][1])
            out_specs[i] = pl.BlockSpec((None, 512, tm), lambda i: (i // nt, 0, i % nt))
        out_shape.append(jax.ShapeDtypeStruct((LANES, n), F32))
        out_specs.append(pl.BlockSpec((LANES, tm), lambda i: (0, i)))
    return pl.pallas_call(
        _hyb_in_kernel, grid=(n // tm,),
        in_specs=[row(d), _const_spec((1, d)), _const_spec(w.shape), _const_spec(wg.shape),
                  _const_spec((1, LANES))],
        out_specs=out_specs, out_shape=out_shape,
        compiler_params=_cparams(("parallel",)),
    )(x, g, w, wg, bias)


def _fox_prep_kernel(fq_ref, fk_ref, gcol_ref, selq_ref, selk_ref, cq_ref, ck_ref,
                     qa_ref, ka_ref, carry_ref):
    @pl.when(pl.program_id(1) == 0)
    def _():
        carry_ref[...] = jnp.zeros_like(carry_ref)

    tm = gcol_ref.shape[0]
    cum = _dot3_left(_tri(tm, "lower"), gcol_ref[...]) + carry_ref[...]
    carry_ref[...] = cum[tm - 1:tm, :]
    hi, mid, lo = _split3(cum * LOG2E)
    xq =jnp.concatenate([fq_ref[...], hi, mid, lo], axis=1)
    xk = jnp.concatenate([fk_ref[...], hi, mid, lo], axis=1)
    qa_ref[...] = (jnp.dot(xq, selq_ref[...], preferred_element_type=F32) + cq_ref[...]).astype(BF16)
    ka_ref[...] = (jnp.dot(xk, selk_ref[...], preferred_element_type=F32) + ck_ref[...]).astype(BF16)


def _fox_aug_constants():
    h = np.arange(FOX_HEADS)
    d = np.arange(FOX_DH)
    rows_qk = (h[:, None] * FOX_DH + d[None, :]).reshape(-1)
    cols_qk = (h[:, None] * LANES + d[None, :]).reshape(-1)
    base = 4 * LANES
    selq = np.zeros((4 * LANES + 3 * LANES, FOX_HEADS * LANES), np.float32)
    selk = np.zeros_like(selq)
    selq[rows_qk, cols_qk] = 1.0
    selk[rows_qk, cols_qk] = 1.0
    cq = np.zeros((1, FOX_HEADS * LANES), np.float32)
    ck = np.zeros_like(cq)
    for part in range(3):
        src = base + part * LANES + 2 * MLSTM_HEADS + h
        selq[src, h * LANES + FOX_DH + part] = 1.0
        selk[src, h * LANES + FOX_DH + 3 + part] = -1.0
        cq[0, h * LANES + FOX_DH + 3 + part] = 1.0
        ck[0, h * LANES + FOX_DH + part] = 1.0
    return jnp.asarray(selq, BF16), jnp.asarray(selk, BF16), jnp.asarray(cq), jnp.asarray(ck)


def _fox_prep(fq, fk16, gcol, b, t, tm):
    n = b * t
    nt = t // tm
    selq, selk, cq, ck = _fox_aug_constants()
    row = lambda c: pl.BlockSpec((tm, c), lambda bi, i: (bi * nt + i, 0))
    width = FOX_HEADS * LANES
    return pl.pallas_call(
        _fox_prep_kernel, grid=(b, nt),
        in_specs=[row(512), row(512), row(LANES), _const_spec(selq.shape), _const_spec(selk.shape),
                  _const_spec(cq.shape), _const_spec(ck.shape)],
        out_specs=[row(width), row(width)],
        out_shape=[jax.ShapeDtypeStruct((n, width), BF16)] * 2,
        scratch_shapes=[pltpu.VMEM((1, LANES), F32)],
        compiler_params=_cparams(("parallel", "arbitrary")),
    )(fq, fk16, gcol, selq, selk, cq, ck)


def _flash_kernel(q_ref, k_ref, vt_ref, o_ref, s_scr, p_scr, m_scr, l_scr, al_scr, acc_scr, *, blk):
    heads = s_scr.shape[0]
    i = pl.program_id(2)
    keep = _iota((blk, blk), 0) <= _iota((blk, blk), 1)
    m_scr[...] = jnp.full(m_scr.shape, NEG, F32)
    l_scr[...] = jnp.zeros(l_scr.shape, F32)
    acc_scr[...] = jnp.zeros(acc_scr.shape, F32)
    sub = LANES
    rows8 = lambda x: x.reshape(sub // 8, 8, LANES)

    def step(j, masked):
        off = pl.multiple_of(j * blk, blk)
        for hh in range(heads):
            hs = slice(hh * LANES, (hh + 1) * LANES)
            st = _dot_nt(k_ref[pl.ds(off, blk), hs], q_ref[:, hs])
            s_scr[hh] = jnp.where(keep, st, NEG) if masked else st
        for hh in range(heads):
            for c in range(blk // LANES):
                cs = slice(c * LANES, (c + 1) * LANES)
                mx8 = jnp.full((8, LANES), NEG, F32)
                for kc in range(blk // sub):
                    mx8 = jnp.maximum(mx8, jnp.max(rows8(s_scr[hh, kc * sub:(kc + 1) * sub, cs]), axis=0))
                m_old = m_scr[hh, :, cs]
                mn = jnp.maximum(m_old, jnp.max(mx8, axis=0, keepdims=True))
                al = jnp.exp2(m_old - mn)
                m_scr[hh, :, cs] = mn
                al_scr[hh, :, cs] = al
                sum8 = jnp.zeros((8, LANES), F32)
                for kc in range(blk // sub):
                    ks = slice(kc * sub, (kc + 1) * sub)
                    p = jnp.exp2(s_scr[hh, ks, cs] - mn)
                    sum8 = sum8 + jnp.sum(rows8(p), axis=0)
                    p_scr[hh, ks, cs] = p.astype(BF16)
                l_scr[hh, :, cs] = al * l_scr[hh, :, cs] + jnp.sum(sum8, axis=0, keepdims=True)
            vt = vt_ref[hh * FOX_DH:(hh + 1) * FOX_DH, pl.ds(off, blk)]
            acc_scr[hh] = al_scr[hh] * acc_scr[hh] + jnp.dot(vt, p_scr[hh], preferred_element_type=F32)

    def body(j, carry):
        step(j, False)
        return carry

    lax.fori_loop(0, i, body, 0)
    step(i, True)
    out = jnp.concatenate([acc_scr[hh] / l_scr[hh] for hh in range(heads)], axis=0)
    o_ref[...] = out.T.astype(BF16)


def _fox_flash(q_aug, k_aug, vt16, b, t, blk):
    nq = t // blk
    hg = FLASH_HEADS
    return pl.pallas_call(
        functools.partial(_flash_kernel, blk=blk), grid=(b, FOX_HEADS // hg, nq),
        in_specs=[pl.BlockSpec((blk, hg * LANES), lambda bi, p, i: (bi * nq + i, p)),
                  pl.BlockSpec((t, hg * LANES), lambda bi, p, i: (bi, p)),
                  pl.BlockSpec((None, hg * FOX_DH, t), lambda bi, p, i: (bi, p, 0))],
        out_specs=pl.BlockSpec((blk, hg * FOX_DH), lambda bi, p, i: (bi * nq + i, p)),
        out_shape=jax.ShapeDtypeStruct((b * t, FOX_HEADS * FOX_DH), BF16),
        scratch_shapes=[pltpu.VMEM((hg, blk, blk), F32), pltpu.VMEM((hg, blk, blk), BF16),
                        pltpu.VMEM((hg, 1, blk), F32), pltpu.VMEM((hg, 1, blk), F32), pltpu.VMEM((hg, 1, blk), F32),
                        pltpu.VMEM((hg, FOX_DH, blk), F32)],
        compiler_params=_cparams(("parallel", "parallel", "arbitrary")),
    )(q_aug, k_aug, vt16)


def _mlstm_chunk_kernel(q_ref, k_ref, v_ref, o_ref, gcol_ref, *rest):
    nb, L = q_ref.shape[0], q_ref.shape[1]
    grow_refs = rest[:nb]
    outg_ref, c0_ref, n0_ref, m0_ref, hm_ref, c_out, n_out, m_out, sc_ref, ms_ref = rest[nb:]
    H, DK, DV = MLSTM_HEADS, MLSTM_DK, MLSTM_DV
    ci = pl.program_id(0)

    @pl.when(ci == 0)
    def _():
        sc_ref[:, :, :, 0:DV] = c0_ref[...]
        sc_ref[:, :, :, DV:] = n0_ref[...]
        ms_ref[...] = m0_ref[...]

    tril, triu = _tri(L, "lower"), _tri(L, "upper")
    causal = _iota((L, L), 0) >= _iota((L, L), 1)
    ones = jnp.ones((L, DV), BF16)
    gcol = [gcol_ref[b] for b in range(nb)]
    grow = [grow_refs[b][...] for b in range(nb)]
    bcol_all = [_dot3_left(tril, gcol[b]) for b in range(nb)]
    brow_all = [_dot3_right(grow[b], triu) for b in range(nb)]
    items = [(b, h) for b in range(nb) for h in range(H)]
    every = range(len(items))
    q = [q_ref[b, :, h * DK:(h + 1) * DK] for b, h in items]
    k = [k_ref[b, :, h * DK:(h + 1) * DK] for b, h in items]
    v1 = [jnp.concatenate([v_ref[b, :, h * DV:(h + 1) * DV], ones], axis=1) for b, h in items]
    sc = [sc_ref[b, h] for b, h in items]
    qk = [_dot_nt(q[i], k[i]) for i in every]
    qs = [_dot(q[i], sc[i]) for i in every]
    s, we, mt, wk, dec, mn = [], [], [], [], [], []
    for i, (b, h) in enumerate(items):
        igc, bc = gcol[b][:, h:h + 1], bcol_all[b][:, H + h:H + h + 1]
        igr, br = grow[b][h:h + 1, :], brow_all[b][H + h:H + h + 1, :]
        mp = ms_ref[b, h:h + 1, 0:1]
        d = jnp.where(causal, bc - br + igr, NEG)
        inter = bc + mp
        mt.append(jnp.maximum(inter, jnp.max(d, axis=1, keepdims=True)))
        we.append(jnp.exp(inter - mt[i]))
        s.append((qk[i] * jnp.exp(d - mt[i])).astype(BF16))
        bl = bc[L - 1:L, :]
        mn.append(jnp.maximum(bl + mp, jnp.max(bl - br + igr, axis=1, keepdims=True)))
        wk.append(jnp.exp(bl - bc + igc - mn[i]))
        dec.append(jnp.exp(bl + mp - mn[i]))
    sv = [jnp.dot(s[i], v1[i], preferred_element_type=F32) for i in every]
    upd = [_dot_tn(k[i].astype(F32) * wk[i], v1[i]) for i in every]
    for i, (b, h) in enumerate(items):
        num = we[i] * qs[i][:, 0:DV] + sv[i][:, 0:DV]
        den = we[i] * qs[i][:, DV:DV + 1] + sv[i][:, DV:DV + 1]
        hh = num / jnp.maximum(jnp.abs(den), jnp.exp(-mt[i]))
        hh = hh * lax.rsqrt(jnp.mean(hh * hh, axis=-1, keepdims=True) + EPS)
        sl = slice(h * DV, (h + 1) * DV)
        hm_ref[b, :, sl] = (hh * outg_ref[:, sl] * _sigmoid(o_ref[b, :, sl])).astype(BF16)
        sc_ref[b, h] = dec[i] * sc[i] + upd[i]
        ms_ref[b, h:h + 1, :] = jnp.broadcast_to(mn[i], (1, LANES))

    @pl.when(ci == pl.num_programs(0) - 1)
    def _():
        c_out[...] = sc_ref[:, :, :, 0:DV]
        n_out[...] = sc_ref[:, :, :, DV:]
        m_out[...] = ms_ref[...]


def _mlstm_chunk(mq, mk, mv, mo, gcol, grow, outg, c0, n0rep, m0rep, b, t):
    L = min(MLSTM_CHUNK, t)
    nc = t // L
    row = lambda c: pl.BlockSpec((b, L, c), lambda ci: (0, ci, 0))
    full = lambda *s: pl.BlockSpec(s, lambda ci: (0,) * len(s))
    H, DK, DV = MLSTM_HEADS, MLSTM_DK, MLSTM_DV
    r3 = lambda a: a.reshape(b, t, a.shape[1])
    hm, c, n, m = pl.pallas_call(
        _mlstm_chunk_kernel, grid=(nc,),
        in_specs=[row(256), row(256), row(512), row(512), row(LANES)]
        + [pl.BlockSpec((8, L), lambda ci, bi=bi: (0, bi * nc + ci)) for bi in range(b)]
        + [_const_spec((1, H * DV)), full(b, H, DK, DV), full(b, H, DK, DV), full(b, 8, LANES)],
        out_specs=[row(H * DV), full(b, H, DK, DV), full(b, H, DK, DV), full(b, 8, LANES)],
        out_shape=[jax.ShapeDtypeStruct((b, t, H * DV), BF16),
                   jax.ShapeDtypeStruct((b, H, DK, DV), F32),
                   jax.ShapeDtypeStruct((b, H, DK, DV), F32),
                   jax.ShapeDtypeStruct((b, 8, LANES), F32)],
        scratch_shapes=[pltpu.VMEM((b, H, DK, 2 * DV), F32), pltpu.VMEM((b, 8, LANES), F32)],
        compiler_params=_cparams(("arbitrary",)),
    )(r3(mq), r3(mk), r3(mv), r3(mo), r3(gcol), *([grow] * b), outg, c0, n0rep, m0rep)
    return hm.reshape(b * t, H * DV), c, n, m


def _ffn_tail(x, mixed, g_ref, wgate_ref, wup_ref, wdown_ref, o_ref, n_chunks):
    x1 = x + _rms(mixed, g_ref[0:1, :])
    h = _rms(x1, g_ref[1:2, :]).astype(BF16)
    ch = wgate_ref.shape[1] // n_chunks
    y = jnp.zeros(x.shape, F32)
    for c in range(n_chunks):
        sl = slice(c * ch, (c + 1) * ch)
        a = jnp.dot(h, wgate_ref[:, sl], preferred_element_type=F32)
        u = jnp.dot(h, wup_ref[:, sl], preferred_element_type=F32)
        act = (a * _sigmoid(a) * u).astype(BF16)
        y = y + jnp.dot(act, wdown_ref[sl, :], preferred_element_type=F32)
    o_ref[...] = x1 + _rms(y, g_ref[2:3, :])


def _hyb_out_kernel(x_ref, hm_ref, hf_ref, wo_ref, g_ref, wgate_ref, wup_ref, wdown_ref, o_ref, *, n_chunks):
    half = hm_ref.shape[1]
    mixed = (jnp.dot(hm_ref[...], wo_ref[0:half, :], preferred_element_type=F32)
             + jnp.dot(hf_ref[...], wo_ref[half:, :], preferred_element_type=F32))
    _ffn_tail(x_ref[...], mixed, g_ref, wgate_ref, wup_ref, wdown_ref, o_ref, n_chunks)


def _seg_sum(x, bd, two_term=False):
    w = bd.shape[0]
    hi = x.astype(BF16)
    lo = (x - hi.astype(F32)).astype(BF16) if two_term else None
    parts = []
    for p in range(x.shape[1] // w):
        sl = slice(p * w, (p + 1) * w)
        part = jnp.dot(hi[:, sl], bd, preferred_element_type=F32)
        if two_term:
            part = part + jnp.dot(lo[:, sl], bd, preferred_element_type=F32)
        parts.append(part)
    return jnp.concatenate(parts, axis=1)


def _block_diag_ones():
    w = 2 * LANES
    r, c = _iota((w, w), 0), _iota((w, w), 1)
    return jnp.where((r // RWKV_DH) == (c // RWKV_DH), 1.0, 0.0).astype(BF16)


def _rwkv_out_kernel(x_ref, y_ref, r_ref, k_ref, v_ref, gate_ref, vec_ref, wo_ref, g_ref,
                     wgate_ref, wup_ref, wdown_ref, o_ref, *, n_chunks, y_transposed):
    bd = _block_diag_ones()
    y = y_ref[...].T if y_transposed else y_ref[...]
    inv = 1.0 / RWKV_DH
    mu = _seg_sum(y, bd, two_term=True) * inv
    yc = y - mu
    var = _seg_sum(yc * yc, bd) * inv
    yn = yc * lax.rsqrt(var + GN_EPS) * vec_ref[0:1, :] + vec_ref[1:2, :]
    r = r_ref[...].astype(F32)
    bonus = _seg_sum(r * k_ref[...].astype(F32) * vec_ref[2:3, :], bd) * v_ref[...].astype(F32)
    z = ((yn + bonus) * gate_ref[...].astype(F32)).astype(BF16)
    mixed = jnp.dot(z, wo_ref[...], preferred_element_type=F32)
    _ffn_tail(x_ref[...], mixed, g_ref, wgate_ref, wup_ref, wdown_ref, o_ref, n_chunks)


def _ffn_chunks(hidden):
    for n in (2, 4, 1):
        if hidden % (n * LANES) == 0:
            return n
    return 1


def _hyb_out(x, hm, hf, wo, g3, wgate, wup, wdown, tm):
    n, d = x.shape
    row = lambda c: pl.BlockSpec((tm, c), lambda i: (i, 0))
    return pl.pallas_call(
        functools.partial(_hyb_out_kernel, n_chunks=_ffn_chunks(wgate.shape[1])), grid=(n // tm,),
        in_specs=[row(d), row(hm.shape[1]), row(hf.shape[1]), _const_spec(wo.shape), _const_spec(g3.shape),
                  _const_spec(wgate.shape), _const_spec(wup.shape), _const_spec(wdown.shape)],
        out_specs=row(d), out_shape=jax.ShapeDtypeStruct((n, d), F32),
        compiler_params=_cparams(("parallel",)),
    )(x, hm, hf, wo, g3, wgate, wup, wdown)


def _rwkv_out(x, y, r, k, v, gate, vec, wo, g3, wgate, wup, wdown, tm, y_transposed=False):
    n, d = x.shape
    assert not y_transposed or tm == n
    row = lambda c: pl.BlockSpec((tm, c), lambda i: (i, 0))
    y_spec = pl.BlockSpec((d, n), lambda i: (0, 0)) if y_transposed else row(d)
    return pl.pallas_call(
        functools.partial(_rwkv_out_kernel, n_chunks=_ffn_chunks(wgate.shape[1]), y_transposed=y_transposed),
        grid=(n // tm,),
        in_specs=[row(d), y_spec] + [row(d)] * 4 + [_const_spec(vec.shape), _const_spec(wo.shape),
                                                    _const_spec(g3.shape), _const_spec(wgate.shape),
                                                    _const_spec(wup.shape), _const_spec(wdown.shape)],
        out_specs=row(d), out_shape=jax.ShapeDtypeStruct((n, d), F32),
        compiler_params=_cparams(("parallel",)),
    )(x, y, r, k, v, gate, vec, wo, g3, wgate, wup, wdown)


def _rwkv_proj_body(h, hprev, vec_ref, wr_ref, wk_ref, wv_ref, w1_ref, w2_ref, a1_ref, a2_ref, g1_ref, g2_ref,
                    r_ref, k_ref, v_ref, na_ref, bb_ref, lw_ref, gate_ref):
    xx = hprev - h
    mixed = lambda i: (h + xx * vec_ref[i:i + 1, :]).astype(BF16)
    r = jnp.dot(mixed(0), wr_ref[...], preferred_element_type=F32)
    k = jnp.dot(mixed(2), wk_ref[...], preferred_element_type=F32)
    v = jnp.dot(mixed(3), wv_ref[...], preferred_element_type=F32)
    wl = vec_ref[6:7, :] + _dot(jnp.tanh(jnp.dot(mixed(1), w1_ref[...], preferred_element_type=F32)), w2_ref[...])
    w_log = -_softplus(-wl) - 0.5
    a = _sigmoid(vec_ref[7:8, :] + _dot(jnp.dot(mixed(4), a1_ref[...], preferred_element_type=F32), a2_ref[...]))
    gate = _dot(_sigmoid(jnp.dot(mixed(5), g1_ref[...], preferred_element_type=F32)), g2_ref[...])
    kk = k * vec_ref[8:9, :]
    nrm = jnp.sqrt(_seg_sum(kk * kk, _block_diag_ones()))
    kk = kk / jnp.maximum(nrm, 1e-12)
    k2 = k * (1.0 + (a - 1.0) * vec_ref[9:10, :])
    lw = -jnp.exp(w_log)
    r_ref[...] = r.astype(BF16)
    k_ref[...] = k2.astype(BF16)
    v_ref[...] = v.astype(BF16)
    na_ref[...] = (-kk).astype(BF16)
    bb_ref[...] = (kk * a).astype(BF16)
    lw_ref[...] = lw
    gate_ref[...] = gate.astype(BF16)
    return r, k2, v, -kk, kk * a, lw


def _rwkv_proj_seq_kernel(x_ref, xprev_ref, shift_ref, g_ref, vec_ref, *rest):
    weights, outs = rest[:9], rest[9:]
    hlast_ref = outs[-1]
    g = g_ref[...]
    h = _rms(x_ref[...], g)
    tm = h.shape[0]
    prev_tile_last = _rms(xprev_ref[...], g)[7:8, :]
    first = jnp.where(pl.program_id(1) == 0, shift_ref[...], prev_tile_last)
    hprev = jnp.where(_iota(h.shape, 0) == 0, first, pltpu.roll(h, 1, 0))
    _rwkv_proj_body(h, hprev, vec_ref, *weights, *outs[:-1])
    hlast_ref[...] = h[tm - 8:tm, :]


def _rwkv_proj_step_kernel(x_ref, shift_ref, g_ref, vec_ref, *rest):
    weights, outs, t_outs = rest[:9], rest[9:17], rest[17:]
    h = _rms(x_ref[...], g_ref[...])
    vals = _rwkv_proj_body(h, shift_ref[...], vec_ref, *weights, *outs[:-1])
    outs[-1][...] = h
    for val, ref in zip(vals, t_outs):
        ref[...] = val.T


_RWKV_PROJ_OUT = (BF16, BF16, BF16, BF16, BF16, F32, BF16)


def _rwkv_proj_seq(x, shift0, g, vec, weights, b, t, tm):
    n, d = x.shape
    nt = t // tm
    row = pl.BlockSpec((tm, d), lambda bi, i: (bi * nt + i, 0))
    prev = pl.BlockSpec((8, d), lambda bi, i: (jnp.maximum((bi * nt + i) * (tm // 8) - 1, 0), 0))
    return pl.pallas_call(
        _rwkv_proj_seq_kernel, grid=(b, nt),
        in_specs=[row, prev, pl.BlockSpec((None, 1, d), lambda bi, i: (bi, 0, 0)), _const_spec((1, d)),
                  _const_spec(vec.shape)] + [_const_spec(w.shape) for w in weights],
        out_specs=[row] * 7 + [pl.BlockSpec((None, 8, d), lambda bi, i: (bi, 0, 0))],
        out_shape=[jax.ShapeDtypeStruct((n, d), dt) for dt in _RWKV_PROJ_OUT]
        + [jax.ShapeDtypeStruct((b, 8, d), F32)],
        compiler_params=_cparams(("parallel", "arbitrary")),
    )(x, x, shift0.reshape(b, 1, d), g, vec, *weights)


def _rwkv_proj_step(x, shift0, g, vec, weights):
    n, d = x.shape
    full = pl.BlockSpec((n, d), lambda i: (0, 0))
    return pl.pallas_call(
        _rwkv_proj_step_kernel, grid=(1,),
        in_specs=[full, full, _const_spec((1, d)), _const_spec(vec.shape)] + [_const_spec(w.shape) for w in weights],
        out_specs=[full] * 8 + [pl.BlockSpec((d, n), lambda i: (0, 0))] * 6,
        out_shape=[jax.ShapeDtypeStruct((n, d), dt) for dt in _RWKV_PROJ_OUT + (F32,)]
        + [jax.ShapeDtypeStruct((d, n), F32)] * 6,
        compiler_params=_cparams(("arbitrary",)),
    )(x, shift0, g, vec, *weights)


def _rwkv_chunk_kernel(r_ref, k_ref, v_ref, na_ref, bb_ref, lw_ref, s0_ref, y_ref, s_out, st_ref):
    nb, L, d = r_ref.shape
    pairs = d // LANES
    items = [(b, p) for b in range(nb) for p in range(pairs)]
    ci = pl.program_id(0)

    @pl.when(ci == 0)
    def _():
        st_ref[...] = s0_ref[...]

    n2 = 2 * L
    rr, cc = _iota((n2, n2), 0), _iota((n2, n2), 1)
    strict = rr > cc
    incl = rr >= cc
    m0 = _iota((L, LANES), 1) < RWKV_DH
    stack = lambda x: jnp.concatenate([jnp.where(m0, x, 0.0), jnp.where(m0, 0.0, x)], axis=0)
    tril = _tri(L, "lower")
    cw = [_dot3_left(tril, lw_ref[b]) for b in range(nb)]
    xs, zs, ze, vs, decay = [], [], [], [], []
    for b, p in items:
        sl = slice(p * LANES, (p + 1) * LANES)
        cwp, lwp = cw[b][:, sl], lw_ref[b, :, sl]
        wl = cwp[L - 1:L, :]
        w_t = jnp.exp(cwp)
        w_inv = jnp.exp(-cwp)
        w_prev = jnp.exp(cwp - lwp)
        w_end = jnp.exp(wl - cwp)
        r, k = r_ref[b, :, sl].astype(F32), k_ref[b, :, sl].astype(F32)
        na, bb = na_ref[b, :, sl].astype(F32), bb_ref[b, :, sl].astype(F32)
        xs.append(jnp.concatenate([stack(na * w_prev), stack(r * w_t)], axis=0).astype(BF16))
        zs.append(jnp.concatenate([stack(bb * w_inv), stack(k * w_inv)], axis=0).astype(BF16))
        ze.append(jnp.concatenate([stack(bb * w_end), stack(k * w_end)], axis=0).astype(BF16))
        vs.append(stack(v_ref[b, :, sl].astype(F32)).astype(BF16))
        decay.append(jnp.exp(wl))
    every = range(len(items))
    st = [st_ref[i] for i in every]
    g = [_dot_nt(xs[i], zs[i]) for i in every]
    xst = [_dot_nt(xs[i], st[i]) for i in every]
    nmat = [jnp.where(strict, g[i][0:n2, 0:n2], 0.0) for i in every]
    a_cat = [jnp.concatenate([jnp.where(strict, g[i][0:n2, n2:], 0.0), jnp.where(incl, g[i][n2:, n2:], 0.0)],
                             axis=0).astype(BF16) for i in every]
    a_rb = [jnp.where(incl, g[i][n2:, 0:n2], 0.0).astype(BF16) for i in every]
    av = [jnp.dot(a_cat[i], vs[i], preferred_element_type=F32) for i in every]
    eye = jnp.where(rr == cc, 1.0, 0.0)
    t = [eye + jnp.where(rr // 2 == cc // 2, nmat[i], 0.0) for i in every]
    bsz = 2
    while bsz < L:
        sel = (rr // (2 * bsz) == cc // (2 * bsz)) & (rr // bsz != cc // bsz)
        tb = [t[i].astype(BF16) for i in every]
        half = [jnp.dot(jnp.where(sel, nmat[i], 0.0).astype(BF16), tb[i], preferred_element_type=F32).astype(BF16)
                for i in every]
        t = [t[i] + jnp.dot(tb[i], half[i], preferred_element_type=F32) for i in every]
        bsz *= 2
    u = [_dot(t[i], xst[i][0:n2] + av[i][0:n2]).astype(BF16) for i in every]
    ys = [xst[i][n2:] + jnp.dot(a_rb[i], u[i], preferred_element_type=F32) + av[i][n2:] for i in every]
    snew = [_dot_tn(jnp.concatenate([u[i], vs[i]], axis=0), ze[i]) for i in every]
    for i, (b, p) in enumerate(items):
        y_ref[b, :, p * LANES:(p + 1) * LANES] = ys[i][0:L] + ys[i][L:]
        st_ref[i] = st[i] * decay[i] + snew[i]

    @pl.when(ci == pl.num_programs(0) - 1)
    def _():
        s_out[...] = st_ref[...]


def _rwkv_chunk(r, k, v, na, bb, lw, s0bd, b, t):
    L = min(RWKV_CHUNK, t)
    d = r.shape[1]
    pairs = d // LANES
    row = pl.BlockSpec((b, L, d), lambda ci: (0, ci, 0))
    st = pl.BlockSpec((b * pairs, LANES, LANES), lambda ci: (0, 0, 0))
    y, s = pl.pallas_call(
        _rwkv_chunk_kernel, grid=(t // L,),
        in_specs=[row] * 6 + [st], out_specs=[row, st],
        out_shape=[jax.ShapeDtypeStruct((b, t, d), F32), jax.ShapeDtypeStruct((b * pairs, LANES, LANES), F32)],
        scratch_shapes=[pltpu.VMEM((b * pairs, LANES, LANES), F32)],
        compiler_params=_cparams(("arbitrary",)),
    )(*(a.reshape(b, t, d) for a in (r, k, v, na, bb, lw)), s0bd.reshape(b * pairs, LANES, LANES))
    return y.reshape(b * t, d), s.reshape(b, pairs, LANES, LANES)


def _fox_decode_kernel(pt_ref, q_ref, knew_ref, vnew_ref, lfnew_ref, *rest, n_pages):
    del pt_ref
    k_refs, v_refs, lf_refs = rest[0:n_pages], rest[n_pages:2 * n_pages], rest[2 * n_pages:3 * n_pages]
    o_ref = rest[3 * n_pages]
    width = FOX_HEADS * FOX_DH
    hmask = (_iota((FOX_HEADS, width), 1) // FOX_DH) == _iota((FOX_HEADS, width), 0)
    qbd = jnp.where(hmask, jnp.broadcast_to(q_ref[...].astype(F32), (FOX_HEADS, width)), 0.0).astype(BF16)
    page = k_refs[0].shape[1]
    later = _tri(page, "strict_upper_t")
    run = lfnew_ref[...]
    logits = [None] * n_pages
    for j in reversed(range(n_pages)):
        lf = lf_refs[j][...]
        logits[j] = _dot(qbd, k_refs[j][...]) + _dot3_right(lf, later) + run
        run = run + jnp.sum(lf, axis=1, keepdims=True)
    s_self = _dot_nt(qbd, jnp.broadcast_to(knew_ref[...], (8, width)))[:, 0:1]
    m = s_self
    for j in range(n_pages):
        m = jnp.maximum(m, jnp.max(logits[j], axis=1, keepdims=True))
    p_self = jnp.exp(s_self - m)
    l = p_self
    acc = p_self * vnew_ref[...]
    for j in range(n_pages):
        p = jnp.exp(logits[j] - m)
        l = l + jnp.sum(p, axis=1, keepdims=True)
        acc = acc + _dot_nt(p, v_refs[j][...])
    o_ref[...] = jnp.sum(jnp.where(hmask, acc / l, 0.0), axis=0, keepdims=True)


def _fox_decode(page_table, q, knew, vnew, lfnew, cache_kt, cache_vt, cache_lf_t):
    nb, n_pages = page_table.shape
    width = q.shape[1]
    page = cache_kt.shape[2]
    row = pl.BlockSpec((None, 1, width), lambda b, pt: (b, 0, 0))
    paged = lambda j, shape: pl.BlockSpec((None,) + shape, lambda b, pt, j=j: (pt[b * n_pages + j], 0, 0))
    in_specs = ([row, row, row, pl.BlockSpec((None, FOX_HEADS, 1), lambda b, pt: (b, 0, 0))]
                + [paged(j, (width, page)) for j in range(n_pages)]
                + [paged(j, (width, page)) for j in range(n_pages)]
                + [paged(j, (FOX_HEADS, page)) for j in range(n_pages)])
    out = pl.pallas_call(
        functools.partial(_fox_decode_kernel, n_pages=n_pages),
        grid_spec=pltpu.PrefetchScalarGridSpec(
            num_scalar_prefetch=1, grid=(nb,), in_specs=in_specs,
            out_specs=pl.BlockSpec((None, 1, width), lambda b, pt: (b, 0, 0))),
        out_shape=jax.ShapeDtypeStruct((nb, 1, width), F32),
        compiler_params=_cparams(("arbitrary",)),
    )(page_table.reshape(-1), q.reshape(nb, 1, width), knew.reshape(nb, 1, width), vnew.reshape(nb, 1, width),
      lfnew.reshape(nb, FOX_HEADS, 1), *([cache_kt] * n_pages), *([cache_vt] * n_pages), *([cache_lf_t] * n_pages))
    return out.reshape(nb, width)


def _columns(rows):
    n = rows.shape[0]
    hi, mid, lo = _split3(rows)
    r, c = _iota((RWKV_DH, RWKV_DH), 0), _iota((RWKV_DH, RWKV_DH), 1)
    eye = jnp.where(r == c, 1.0, 0.0).astype(BF16)
    t = lax.dot_general(eye, jnp.concatenate([hi, mid, lo], axis=0), (((1,), (1,)), ((), ())),
                        preferred_element_type=F32)
    return (t[:, 2 * n:3 * n] + t[:, n:2 * n]) + t[:, 0:n]


def _mlstm_step_kernel(q_ref, k_ref, v_ref, o_ref, gs_ref, m0_ref, outg_ref, c0_ref, n0_ref,
                       hm_ref, c_out, n_out, m_out):
    kf = k_ref[...].astype(F32)
    kcols = _columns(jnp.concatenate([kf, jnp.zeros((8 - MLSTM_HEADS, MLSTM_DK), F32)], axis=0))
    gs = gs_ref[...]
    m0 = m0_ref[...]
    for h in range(MLSTM_HEADS):
        q, k, v = q_ref[h:h + 1, :].astype(F32), kf[h:h + 1, :], v_ref[h:h + 1, :].astype(F32)
        ig, lf, mp = gs[:, h:h + 1], gs[:, MLSTM_HEADS + h:MLSTM_HEADS + h + 1], m0[:, h:h + 1]
        inter = lf + mp
        mt = jnp.maximum(inter, ig)
        wi = jnp.exp(ig - mt)
        we = jnp.exp(inter - mt)
        s = jnp.sum(q * k, axis=1, keepdims=True) * wi
        c = c0_ref[h]
        n = n0_ref[h:h + 1, :]
        qc = _dot(jnp.broadcast_to(q, (8, MLSTM_DK)), c)[0:1, :]
        num = we * qc + s * v
        den = we * jnp.sum(q * n, axis=1, keepdims=True) + s
        hh = num / jnp.maximum(jnp.abs(den), jnp.exp(-mt))
        hh = hh * lax.rsqrt(jnp.mean(hh * hh, axis=-1, keepdims=True) + EPS)
        sl = slice(h * MLSTM_DV, (h + 1) * MLSTM_DV)
        hm_ref[:, sl] = hh * outg_ref[:, sl] * _sigmoid(o_ref[:, sl])
        c_out[h] = we * c + (wi * kcols[:, h:h + 1]) * v
        n_out[h:h + 1, :] = we * n + wi * k
        m_out[:, h:h + 1] = mt


def _mlstm_step(mq, mk, mv, mo, gcol, outg, c0, n0, m0):
    nb = mq.shape[0]
    H, DK, DV = MLSTM_HEADS, MLSTM_DK, MLSTM_DV
    per = lambda *s: pl.BlockSpec((None,) + s, lambda b: (b,) + (0,) * len(s))
    hm, c, n, m = pl.pallas_call(
        _mlstm_step_kernel, grid=(nb,),
        in_specs=[per(H, DK), per(H, DK), per(H, DV), per(1, H * DV), per(1, LANES), per(1, H),
                  _const_spec((1, H * DV)), per(H, DK, DV), per(H, DK)],
        out_specs=[per(1, H * DV), per(H, DK, DV), per(H, DK), per(1, H)],
        out_shape=[jax.ShapeDtypeStruct((nb, 1, H * DV), F32), jax.ShapeDtypeStruct((nb, H, DK, DV), F32),
                   jax.ShapeDtypeStruct((nb, H, DK), F32), jax.ShapeDtypeStruct((nb, 1, H), F32)],
        compiler_params=_cparams(("arbitrary",)),
    )(mq.reshape(nb, H, DK), mk.reshape(nb, H, DK), mv.reshape(nb, H, DV), mo.reshape(nb, 1, H * DV),
      gcol.reshape(nb, 1, LANES), m0.reshape(nb, 1, H), outg, c0, n0)
    return hm.reshape(nb, H * DV), c, n, m.reshape(nb, H)


def _rwkv_step_kernel(r_ref, k_ref, v_ref, na_ref, bb_ref, lw_ref, s0_ref, y_ref, s_out):
    w, a, bv, kv, rv = jnp.exp(lw_ref[...]), na_ref[...], bb_ref[...], k_ref[...], r_ref[...]

    def group(g8, carry):
        base = pl.multiple_of(g8 * 8, 8)
        vrows = v_ref[pl.ds(base, 8), :]
        ys = []
        for i in range(8):
            s = s0_ref[base + i]
            sa = jnp.sum(s * a, axis=0, keepdims=True)
            sn = s * w + sa * bv + vrows[i:i + 1, :] * kv
            s_out[base + i] = sn
            ys.append(jnp.sum(sn * rv, axis=0, keepdims=True))
        y_ref[pl.ds(base, 8), :] = jnp.concatenate(ys, axis=0)
        return carry

    lax.fori_loop(0, RWKV_DH // 8, group, 0)


def _rwkv_step(r_t, k_t, v_t, na_t, bb_t, lw_t, s0_t):
    d, nb = r_t.shape
    H, N = RWKV_HEADS, RWKV_DH
    vec = pl.BlockSpec((N, nb), lambda h: (h, 0))
    st = pl.BlockSpec((None, N, N, nb), lambda h: (h, 0, 0, 0))
    return pl.pallas_call(
        _rwkv_step_kernel, grid=(H,),
        in_specs=[vec] * 6 + [st], out_specs=[vec, st],
        out_shape=[jax.ShapeDtypeStruct((d, nb), F32), jax.ShapeDtypeStruct((H, N, N, nb), F32)],
        compiler_params=_cparams(("parallel",)),
    )(r_t, k_t, v_t, na_t, bb_t, lw_t, s0_t)


def _pair_block_diag_inv(sbd, n):
    return jnp.stack([sbd[:, :, 0:n, 0:n], sbd[:, :, n:, n:]], axis=2).reshape(sbd.shape[0], -1, n, n)


def kernel(x_prompt, x_sample, cache_fox_k, cache_fox_v, cache_fox_lf, state_mlstm_c, state_mlstm_n, state_mlstm_m, state_rwkv_s, state_rwkv_shift, page_table, norm_g, hyb_w_in, mlstm_i_bias, mlstm_f_bias, mlstm_out_g, fox_f_bias, hyb_w_out, rwkv_mix, rwkv_w_r, rwkv_w_k, rwkv_w_v, rwkv_w_o, rwkv_w0, rwkv_w1, rwkv_w2, rwkv_a0, rwkv_a1, rwkv_a2, rwkv_g1, rwkv_g2, rwkv_k_k, rwkv_k_a, rwkv_r_k, rwkv_ln_w, rwkv_ln_b, ffn_w_gate, ffn_w_up, ffn_w_down):
    bp, t, d = x_prompt.shape
    bs = x_sample.shape[0]
    assert x_sample.shape[1] == 1 and norm_g.shape[0] == 2
    n_phys, page = cache_fox_k.shape[1], cache_fox_k.shape[2]
    H, DK, DV = MLSTM_HEADS, MLSTM_DK, MLSTM_DV
    b16 = lambda a: a.astype(BF16)

    w_in = hyb_w_in[0]
    g_lo = 2 * H * DK + 2 * H * DV
    w_main = b16(jnp.concatenate([w_in[:, :g_lo], w_in[:, g_lo + 2 * H:g_lo + 2 * H + 3 * FOX_HEADS * FOX_DH]], axis=1))
    w_gate_cols = jnp.concatenate([w_in[:, g_lo:g_lo + 2 * H], w_in[:, -FOX_HEADS:]], axis=1)
    n_g = w_gate_cols.shape[1]
    w_g = b16(jnp.pad(w_gate_cols, ((0, 0), (0, LANES - n_g))))
    gate_bias = jnp.pad(jnp.concatenate([mlstm_i_bias[0], mlstm_f_bias[0], fox_f_bias[0]]), (0, LANES - n_g))[None, :]
    outg = mlstm_out_g[0][None, :]
    w_out = b16(hyb_w_out[0])
    ffn = [(b16(ffn_w_gate[l]), b16(ffn_w_up[l]), b16(ffn_w_down[l])) for l in range(2)]
    rw = [b16(a[0]) for a in (rwkv_w_r, rwkv_w_k, rwkv_w_v, rwkv_w1, rwkv_w2, rwkv_a1, rwkv_a2, rwkv_g1, rwkv_g2)]
    proj_vec = jnp.concatenate([rwkv_mix[0], rwkv_w0, rwkv_a0, rwkv_k_k, rwkv_k_a,
                                jnp.zeros((6, d), F32)], axis=0)
    out_vec = jnp.concatenate([rwkv_ln_w, rwkv_ln_b, rwkv_r_k.reshape(1, d), jnp.zeros((5, d), F32)], axis=0)
    w_o = b16(rwkv_w_o[0])

    def even_in(x, tm, with_rows, seq_len=None):
        return _hyb_in(x, norm_g[0, 0][None, :], w_main, w_g, gate_bias, tm, with_rows, seq_len)

    n = bp * t
    tm = min(ROW_TILE, t)
    xp = x_prompt.reshape(n, d)
    mq, mk, mv, mo, fq, fk_t, fv_t, fk16, fvt16, gcol, grow = even_in(xp, tm, True, t)
    q_aug, k_aug = _fox_prep(fq, fk16, gcol, bp, t, tm)
    hf = _fox_flash(q_aug, k_aug, fvt16, bp, t, min(ATTN_BLOCK, t))
    zc = jnp.zeros((bp, H, DK, DV), F32)
    hm, pc, pn_rep, pm_rep = _mlstm_chunk(mq, mk, mv, mo, gcol, grow, outg, zc, zc,
                                          jnp.zeros((bp, 8, LANES), F32), bp, t)
    x1 = _hyb_out(xp, hm, hf, w_out, norm_g[0, 1:4], *ffn[0], tm)
    r, k2, v, na, bb, lw, gate, hlast = _rwkv_proj_seq(x1, jnp.zeros((bp, d), F32), norm_g[1, 0][None, :],
                                                       proj_vec, rw, bp, t, tm)
    s0p = jnp.zeros((bp, RWKV_HEADS // 2, LANES, LANES), F32)
    y, ps_bd = _rwkv_chunk(r, k2, v, na, bb, lw, s0p, bp, t)
    y_prompt = _rwkv_out(x1, y, r, k2, v, gate, out_vec, w_o, norm_g[1, 1:4], *ffn[1], tm).reshape(bp, t, d)
    p_rwkv_s = _pair_block_diag_inv(ps_bd, RWKV_DH)

    xs = x_sample.reshape(bs, d)
    smq, smk, smv, smo, sfq, sfk, sfv, _, _, sgcol = even_in(xs, bs, False)
    lff_new = sgcol[:, 2 * H:2 * H + FOX_HEADS]
    shf = _fox_decode(page_table, sfq, sfk, sfv, lff_new,
                      jnp.transpose(cache_fox_k[0], (0, 2, 3, 1)).reshape(n_phys, -1, page),
                      jnp.transpose(cache_fox_v[0], (0, 2, 3, 1)).reshape(n_phys, -1, page),
                      jnp.swapaxes(cache_fox_lf[0], 1, 2))
    shm, sc, sn, sm = _mlstm_step(smq, smk, smv, smo, sgcol, outg, state_mlstm_c[0], state_mlstm_n[0],
                                  state_mlstm_m[0])
    xs1 = _hyb_out(xs, b16(shm), b16(shf), w_out, norm_g[0, 1:4], *ffn[0], bs)
    sr, sk2, sv, _, _, _, sgate, sh, *step_in = _rwkv_proj_step(xs1, state_rwkv_shift[0], norm_g[1, 0][None, :],
                                                                proj_vec, rw)
    sy_t, ss_t = _rwkv_step(*step_in, jnp.transpose(state_rwkv_s[0], (1, 2, 3, 0)))
    ss = jnp.transpose(ss_t, (3, 0, 1, 2))
    y_sample = _rwkv_out(xs1, sy_t, sr, sk2, sv, sgate, out_vec, w_o, norm_g[1, 1:4], *ffn[1], bs,
                         y_transposed=True).reshape(bs, 1, d)

    e = lambda a: a[None]
    return (y_prompt, y_sample,
            e(jnp.transpose(fk_t.reshape(bp, FOX_HEADS, FOX_DH, t), (0, 3, 1, 2))),
            e(jnp.transpose(fv_t.reshape(bp, FOX_HEADS, FOX_DH, t), (0, 3, 1, 2))),
            e(jnp.transpose(grow[2 * H:2 * H + FOX_HEADS].reshape(FOX_HEADS, bp, t), (1, 2, 0))),
            e(pc), e(pn_rep[..., 0]), e(pm_rep[:, :H, 0]), e(p_rwkv_s), e(hlast[:, 7, :]),
            e(sfk.reshape(bs, 1, FOX_HEADS, FOX_DH)), e(sfv.reshape(bs, 1, FOX_HEADS, FOX_DH)),
            e(lff_new.reshape(bs, 1, FOX_HEADS)),
            e(sc), e(sn), e(sm), e(ss), e(sh))
```

```python
import functools

import jax
import jax.numpy as jnp
import numpy as np
from jax import lax
from jax.experimental import pallas as pl
from jax.experimental.pallas import tpu as pltpu

F32, BF16 = jnp.float32, jnp.bfloat16
EPS = 1e-6
GN_EPS = 64e-5
NEG = -1e30
LOG2E = 1.4426950408889634

MLSTM_HEADS, MLSTM_DK, MLSTM_DV = 4, 64, 128
FOX_HEADS, FOX_DH = 8, 64
RWKV_HEADS, RWKV_DH = 16, 64
MLSTM_CHUNK = 128
RWKV_CHUNK = 64
LANES = 128
ATTN_BLOCK = 512
FLASH_HEADS = 4
ROW_TILE = 512
VMEM_LIMIT = 56 * 1024 * 1024


def _cparams(sem):
    return pltpu.CompilerParams(dimension_semantics=sem, vmem_limit_bytes=VMEM_LIMIT)


def _const_spec(shape):
    nd = len(shape)
    return pl.BlockSpec(shape, lambda *a: (0,) * nd, pipeline_mode=pl.Buffered(1))


def _dot(a, b):
    return jnp.dot(a.astype(BF16), b.astype(BF16), preferred_element_type=F32)


def _dot_nt(a, b):
    return lax.dot_general(a.astype(BF16), b.astype(BF16), (((1,), (1,)), ((), ())),
                           preferred_element_type=F32)


def _dot_tn(a, b):
    return lax.dot_general(a.astype(BF16), b.astype(BF16), (((0,), (0,)), ((), ())),
                           preferred_element_type=F32)


def _split3(x):
    hi = x.astype(BF16)
    r1 = x - hi.astype(F32)
    mid = r1.astype(BF16)
    lo = (r1 - mid.astype(F32)).astype(BF16)
    return hi, mid, lo


def _dot3_left(m, x):
    hi, mid, lo = _split3(x)
    d = lambda t: jnp.dot(m, t, preferred_element_type=F32)
    return (d(lo) + d(mid)) + d(hi)


def _dot3_right(x, m):
    hi, mid, lo = _split3(x)
    d = lambda t: jnp.dot(t, m, preferred_element_type=F32)
    return (d(lo) + d(mid)) + d(hi)


def _rms(x, g):
    return x * lax.rsqrt(jnp.mean(x * x, axis=-1, keepdims=True) + EPS) * g


def _sigmoid(x):
    return 1.0 / (1.0 + jnp.exp(-x))


def _log_sigmoid(x):
    return jnp.minimum(x, 0.0) - jnp.log1p(jnp.exp(-jnp.abs(x)))


def _softplus(x):
    return jnp.maximum(x, 0.0) + jnp.log1p(jnp.exp(-jnp.abs(x)))


def _iota(shape, dim):
    return lax.broadcasted_iota(jnp.int32, shape, dim)


def _tri(n, kind):
    r, c = _iota((n, n), 0), _iota((n, n), 1)
    m = {"lower": r >= c, "upper": r <= c, "strict_upper_t": r > c}[kind]
    return jnp.where(m, 1.0, 0.0).astype(BF16)


def _hyb_in_kernel(x_ref, g_ref, w_ref, wg_ref, bias_ref,
                   mq_ref, mk_ref, mv_ref, mo_ref, fq_ref, fk_ref, fv_ref, fk16_ref, fv16_ref,
                   gcol_ref, *maybe_grow):
    hn = _rms(x_ref[...], g_ref[...]).astype(BF16)
    seg = lambda lo, hi: jnp.dot(hn, w_ref[:, lo:hi], preferred_element_type=F32)
    mq_ref[...] = seg(0, 256).astype(BF16)
    mk_ref[...] = (seg(256, 512) * (MLSTM_DK ** -0.5)).astype(BF16)
    mv_ref[...] = seg(512, 1024).astype(BF16)
    mo_ref[...] = seg(1024, 1536)
    q_scale = FOX_DH ** -0.5 * (LOG2E if maybe_grow else 1.0)
    fq_ref[...] = (seg(1536, 2048) * q_scale).astype(BF16)
    fk = seg(2048, 2560)
    fv = seg(2560, 3072)
    fk_ref[...] = fk.T if maybe_grow else fk
    fv_ref[...] = fv.T if maybe_grow else fv
    fk16_ref[...] = fk.astype(BF16)
    fv16_ref[...] = (fv.T if maybe_grow else fv).astype(BF16)
    gates = jnp.dot(hn, wg_ref[...], preferred_element_type=F32) + bias_ref[...]
    lane = _iota(gates.shape, 1)
    gc = jnp.where(lane < MLSTM_HEADS, gates, _log_sigmoid(gates))
    gcol_ref[...] = gc
    if maybe_grow:
        maybe_grow[0][...] = gc.T


def _hyb_in(x, g, w, wg, bias, tm, with_rows, seq_len=None):
    n, d = x.shape
    row = lambda c: pl.BlockSpec((tm, c), lambda i: (i, 0))
    outs = [(256, BF16), (256, BF16), (512, BF16), (512, F32), (512, BF16), (512, F32), (512, F32),
            (512, BF16), (512, BF16), (LANES, F32)]
    out_shape = [jax.ShapeDtypeStruct((n, c), t) for c, t in outs]
    out_specs = [row(c) for c, _ in outs]
    if with_rows:
        nt = seq_len // tm
        for i in (5, 6, 8):
            out_shape[i] = jax.ShapeDtypeStruct((n // seq_len, 512, seq_len), outs[i][1])
            out_specs[i] = pl.BlockSpec((None, 512, tm), lambda i: (i // nt, 0, i % nt))
        out_shape.append(jax.ShapeDtypeStruct((LANES, n), F32))
        out_specs.append(pl.BlockSpec((LANES, tm), lambda i: (0, i)))
    return pl.pallas_call(
        _hyb_in_kernel, grid=(n // tm,),
        in_specs=[row(d), _const_spec((1, d)), _const_spec(w.shape), _const_spec(wg.shape),
                  _const_spec((1, LANES))],
        out_specs=out_specs, out_shape=out_shape,
        compiler_params=_cparams(("parallel",)),
    )(x, g, w, wg, bias)


def _fox_prep_kernel(fq_ref, fk_ref, gcol_ref, selq_ref, selk_ref, cq_ref, ck_ref,
                     qa_ref, ka_ref, carry_ref):
    @pl.when(pl.program_id(1) == 0)
    def _():
        carry_ref[...] = jnp.zeros_like(carry_ref)

    tm = gcol_ref.shape[0]
    cum = _dot3_left(_tri(tm, "lower"), gcol_ref[...]) + carry_ref[...]
    carry_ref[...] = cum[tm - 1:tm, :]
    hi, mid, lo = _split3(cum * LOG2E)
    xq =jnp.concatenate([fq_ref[...], hi, mid, lo], axis=1)
    xk = jnp.concatenate([fk_ref[...], hi, mid, lo], axis=1)
    qa_ref[...] = (jnp.dot(xq, selq_ref[...], preferred_element_type=F32) + cq_ref[...]).astype(BF16)
    ka_ref[...] = (jnp.dot(xk, selk_ref[...], preferred_element_type=F32) + ck_ref[...]).astype(BF16)


def _fox_aug_constants():
    h = np.arange(FOX_HEADS)
    d = np.arange(FOX_DH)
    rows_qk = (h[:, None] * FOX_DH + d[None, :]).reshape(-1)
    cols_qk = (h[:, None] * LANES + d[None, :]).reshape(-1)
    base = 4 * LANES
    selq = np.zeros((4 * LANES + 3 * LANES, FOX_HEADS * LANES), np.float32)
    selk = np.zeros_like(selq)
    selq[rows_qk, cols_qk] = 1.0
    selk[rows_qk, cols_qk] = 1.0
    cq = np.zeros((1, FOX_HEADS * LANES), np.float32)
    ck = np.zeros_like(cq)
    for part in range(3):
        src = base + part * LANES + 2 * MLSTM_HEADS + h
        selq[src, h * LANES + FOX_DH + part] = 1.0
        selk[src, h * LANES + FOX_DH + 3 + part] = -1.0
        cq[0, h * LANES + FOX_DH + 3 + part] = 1.0
        ck[0, h * LANES + FOX_DH + part] = 1.0
    return jnp.asarray(selq, BF16), jnp.asarray(selk, BF16), jnp.asarray(cq), jnp.asarray(ck)


def _fox_prep(fq, fk16, gcol, b, t, tm):
    n = b * t
    nt = t // tm
    selq, selk, cq, ck = _fox_aug_constants()
    row = lambda c: pl.BlockSpec((tm, c), lambda bi, i: (bi * nt + i, 0))
    width = FOX_HEADS * LANES
    return pl.pallas_call(
        _fox_prep_kernel, grid=(b, nt),
        in_specs=[row(512), row(512), row(LANES), _const_spec(selq.shape), _const_spec(selk.shape),
                  _const_spec(cq.shape), _const_spec(ck.shape)],
        out_specs=[row(width), row(width)],
        out_shape=[jax.ShapeDtypeStruct((n, width), BF16)] * 2,
        scratch_shapes=[pltpu.VMEM((1, LANES), F32)],
        compiler_params=_cparams(("parallel", "arbitrary")),
    )(fq, fk16, gcol, selq, selk, cq, ck)


def _flash_kernel(q_ref, k_ref, vt_ref, o_ref, s_scr, p_scr, m_scr, l_scr, al_scr, acc_scr, *, blk):
    heads = s_scr.shape[0]
    i = pl.program_id(2)
    keep = _iota((blk, blk), 0) <= _iota((blk, blk), 1)
    m_scr[...] = jnp.full(m_scr.shape, NEG, F32)
    l_scr[...] = jnp.zeros(l_scr.shape, F32)
    acc_scr[...] = jnp.zeros(acc_scr.shape, F32)
    sub = LANES
    rows8 = lambda x: x.reshape(sub // 8, 8, LANES)

    def step(j, masked):
        off = pl.multiple_of(j * blk, blk)
        for hh in range(heads):
            hs = slice(hh * LANES, (hh + 1) * LANES)
            st = _dot_nt(k_ref[pl.ds(off, blk), hs], q_ref[:, hs])
            s_scr[hh] = jnp.where(keep, st, NEG) if masked else st
        for hh in range(heads):
            for c in range(blk // LANES):
                cs = slice(c * LANES, (c + 1) * LANES)
                mx8 = jnp.full((8, LANES), NEG, F32)
                for kc in range(blk // sub):
                    mx8 = jnp.maximum(mx8, jnp.max(rows8(s_scr[hh, kc * sub:(kc + 1) * sub, cs]), axis=0))
                m_old = m_scr[hh, :, cs]
                mn = jnp.maximum(m_old, jnp.max(mx8, axis=0, keepdims=True))
                al = jnp.exp2(m_old - mn)
                m_scr[hh, :, cs] = mn
                al_scr[hh, :, cs] = al
                sum8 = jnp.zeros((8, LANES), F32)
                for kc in range(blk // sub):
                    ks = slice(kc * sub, (kc + 1) * sub)
                    p = jnp.exp2(s_scr[hh, ks, cs] - mn)
                    sum8 = sum8 + jnp.sum(rows8(p), axis=0)
                    p_scr[hh, ks, cs] = p.astype(BF16)
                l_scr[hh, :, cs] = al * l_scr[hh, :, cs] + jnp.sum(sum8, axis=0, keepdims=True)
            vt = vt_ref[hh * FOX_DH:(hh + 1) * FOX_DH, pl.ds(off, blk)]
            acc_scr[hh] = al_scr[hh] * acc_scr[hh] + jnp.dot(vt, p_scr[hh], preferred_element_type=F32)

    def body(j, carry):
        step(j, False)
        return carry

    lax.fori_loop(0, i, body, 0)
    step(i, True)
    out = jnp.concatenate([acc_scr[hh] / l_scr[hh] for hh in range(heads)], axis=0)
    o_ref[...] = out.T.astype(BF16)


def _fox_flash(q_aug, k_aug, vt16, b, t, blk):
    nq = t // blk
    hg = FLASH_HEADS
    return pl.pallas_call(
        functools.partial(_flash_kernel, blk=blk), grid=(b, FOX_HEADS // hg, nq),
        in_specs=[pl.BlockSpec((blk, hg * LANES), lambda bi, p, i: (bi * nq + i, p)),
                  pl.BlockSpec((t, hg * LANES), lambda bi, p, i: (bi, p)),
                  pl.BlockSpec((None, hg * FOX_DH, t), lambda bi, p, i: (bi, p, 0))],
        out_specs=pl.BlockSpec((blk, hg * FOX_DH), lambda bi, p, i: (bi * nq + i, p)),
        out_shape=jax.ShapeDtypeStruct((b * t, FOX_HEADS * FOX_DH), BF16),
        scratch_shapes=[pltpu.VMEM((hg, blk, blk), F32), pltpu.VMEM((hg, blk, blk), BF16),
                        pltpu.VMEM((hg, 1, blk), F32), pltpu.VMEM((hg, 1, blk), F32), pltpu.VMEM((hg, 1, blk), F32),
                        pltpu.VMEM((hg, FOX_DH, blk), F32)],
        compiler_params=_cparams(("parallel", "parallel", "arbitrary")),
    )(q_aug, k_aug, vt16)


def _mlstm_chunk_kernel(q_ref, k_ref, v_ref, o_ref, gcol_ref, *rest):
    nb, L = q_ref.shape[0], q_ref.shape[1]
    grow_refs = rest[:nb]
    outg_ref, c0_ref, n0_ref, m0_ref, hm_ref, c_out, n_out, m_out, sc_ref, ms_ref = rest[nb:]
    H, DK, DV = MLSTM_HEADS, MLSTM_DK, MLSTM_DV
    ci = pl.program_id(0)

    @pl.when(ci == 0)
    def _():
        sc_ref[:, :, :, 0:DV] = c0_ref[...]
        sc_ref[:, :, :, DV:] = n0_ref[...]
        ms_ref[...] = m0_ref[...]

    tril, triu = _tri(L, "lower"), _tri(L, "upper")
    causal = _iota((L, L), 0) >= _iota((L, L), 1)
    ones = jnp.ones((L, DV), BF16)
    gcol = [gcol_ref[b] for b in range(nb)]
    grow = [grow_refs[b][...] for b in range(nb)]
    bcol_all = [_dot3_left(tril, gcol[b]) for b in range(nb)]
    brow_all = [_dot3_right(grow[b], triu) for b in range(nb)]
    items = [(b, h) for b in range(nb) for h in range(H)]
    every = range(len(items))
    q = [q_ref[b, :, h * DK:(h + 1) * DK] for b, h in items]
    k = [k_ref[b, :, h * DK:(h + 1) * DK] for b, h in items]
    v1 = [jnp.concatenate([v_ref[b, :, h * DV:(h + 1) * DV], ones], axis=1) for b, h in items]
    sc = [sc_ref[b, h] for b, h in items]
    qk = [_dot_nt(q[i], k[i]) for i in every]
    qs = [_dot(q[i], sc[i]) for i in every]
    s, we, mt, wk, dec, mn = [], [], [], [], [], []
    for i, (b, h) in enumerate(items):
        igc, bc = gcol[b][:, h:h + 1], bcol_all[b][:, H + h:H + h + 1]
        igr, br = grow[b][h:h + 1, :], brow_all[b][H + h:H + h + 1, :]
        mp = ms_ref[b, h:h + 1, 0:1]
        d = jnp.where(causal, bc - br + igr, NEG)
        inter = bc + mp
        mt.append(jnp.maximum(inter, jnp.max(d, axis=1, keepdims=True)))
        we.append(jnp.exp(inter - mt[i]))
        s.append((qk[i] * jnp.exp(d - mt[i])).astype(BF16))
        bl = bc[L - 1:L, :]
        mn.append(jnp.maximum(bl + mp, jnp.max(bl - br + igr, axis=1, keepdims=True)))
        wk.append(jnp.exp(bl - bc + igc - mn[i]))
        dec.append(jnp.exp(bl + mp - mn[i]))
    sv = [jnp.dot(s[i], v1[i], preferred_element_type=F32) for i in every]
    upd = [_dot_tn(k[i].astype(F32) * wk[i], v1[i]) for i in every]
    for i, (b, h) in enumerate(items):
        num = we[i] * qs[i][:, 0:DV] + sv[i][:, 0:DV]
        den = we[i] * qs[i][:, DV:DV + 1] + sv[i][:, DV:DV + 1]
        hh = num / jnp.maximum(jnp.abs(den), jnp.exp(-mt[i]))
        hh = hh * lax.rsqrt(jnp.mean(hh * hh, axis=-1, keepdims=True) + EPS)
        sl = slice(h * DV, (h + 1) * DV)
        hm_ref[b, :, sl] = (hh * outg_ref[:, sl] * _sigmoid(o_ref[b, :, sl])).astype(BF16)
        sc_ref[b, h] = dec[i] * sc[i] + upd[i]
        ms_ref[b, h:h + 1, :] = jnp.broadcast_to(mn[i], (1, LANES))

    @pl.when(ci == pl.num_programs(0) - 1)
    def _():
        c_out[...] = sc_ref[:, :, :, 0:DV]
        n_out[...] = sc_ref[:, :, :, DV:]
        m_out[...] = ms_ref[...]


def _mlstm_chunk(mq, mk, mv, mo, gcol, grow, outg, c0, n0rep, m0rep, b, t):
    L = min(MLSTM_CHUNK, t)
    nc = t // L
    row = lambda c: pl.BlockSpec((b, L, c), lambda ci: (0, ci, 0))
    full = lambda *s: pl.BlockSpec(s, lambda ci: (0,) * len(s))
    H, DK, DV = MLSTM_HEADS, MLSTM_DK, MLSTM_DV
    r3 = lambda a: a.reshape(b, t, a.shape[1])
    hm, c, n, m = pl.pallas_call(
        _mlstm_chunk_kernel, grid=(nc,),
        in_specs=[row(256), row(256), row(512), row(512), row(LANES)]
        + [pl.BlockSpec((8, L), lambda ci, bi=bi: (0, bi * nc + ci)) for bi in range(b)]
        + [_const_spec((1, H * DV)), full(b, H, DK, DV), full(b, H, DK, DV), full(b, 8, LANES)],
        out_specs=[row(H * DV), full(b, H, DK, DV), full(b, H, DK, DV), full(b, 8, LANES)],
        out_shape=[jax.ShapeDtypeStruct((b, t, H * DV), BF16),
                   jax.ShapeDtypeStruct((b, H, DK, DV), F32),
                   jax.ShapeDtypeStruct((b, H, DK, DV), F32),
                   jax.ShapeDtypeStruct((b, 8, LANES), F32)],
        scratch_shapes=[pltpu.VMEM((b, H, DK, 2 * DV), F32), pltpu.VMEM((b, 8, LANES), F32)],
        compiler_params=_cparams(("arbitrary",)),
    )(r3(mq), r3(mk), r3(mv), r3(mo), r3(gcol), *([grow] * b), outg, c0, n0rep, m0rep)
    return hm.reshape(b * t, H * DV), c, n, m


def _ffn_tail(x, mixed, g_ref, wgate_ref, wup_ref, wdown_ref, o_ref, n_chunks):
    x1 = x + _rms(mixed, g_ref[0:1, :])
    h = _rms(x1, g_ref[1:2, :]).astype(BF16)
    ch = wgate_ref.shape[1] // n_chunks
    y = jnp.zeros(x.shape, F32)
    for c in range(n_chunks):
        sl = slice(c * ch, (c + 1) * ch)
        a = jnp.dot(h, wgate_ref[:, sl], preferred_element_type=F32)
        u = jnp.dot(h, wup_ref[:, sl], preferred_element_type=F32)
        act = (a * _sigmoid(a) * u).astype(BF16)
        y = y + jnp.dot(act, wdown_ref[sl, :], preferred_element_type=F32)
    o_ref[...] = x1 + _rms(y, g_ref[2:3, :])


def _hyb_out_kernel(x_ref, hm_ref, hf_ref, wo_ref, g_ref, wgate_ref, wup_ref, wdown_ref, o_ref, *, n_chunks):
    half = hm_ref.shape[1]
    mixed = (jnp.dot(hm_ref[...], wo_ref[0:half, :], preferred_element_type=F32)
             + jnp.dot(hf_ref[...], wo_ref[half:, :], preferred_element_type=F32))
    _ffn_tail(x_ref[...], mixed, g_ref, wgate_ref, wup_ref, wdown_ref, o_ref, n_chunks)


def _seg_sum(x, bd, two_term=False):
    w = bd.shape[0]
    hi = x.astype(BF16)
    lo = (x - hi.astype(F32)).astype(BF16) if two_term else None
    parts = []
    for p in range(x.shape[1] // w):
        sl = slice(p * w, (p + 1) * w)
        part = jnp.dot(hi[:, sl], bd, preferred_element_type=F32)
        if two_term:
            part = part + jnp.dot(lo[:, sl], bd, preferred_element_type=F32)
        parts.append(part)
    return jnp.concatenate(parts, axis=1)


def _block_diag_ones():
    w = 2 * LANES
    r, c = _iota((w, w), 0), _iota((w, w), 1)
    return jnp.where((r // RWKV_DH) == (c // RWKV_DH), 1.0, 0.0).astype(BF16)


def _rwkv_out_kernel(x_ref, y_ref, r_ref, k_ref, v_ref, gate_ref, vec_ref, wo_ref, g_ref,
                     wgate_ref, wup_ref, wdown_ref, o_ref, *, n_chunks, y_transposed):
    bd = _block_diag_ones()
    y = y_ref[...].T if y_transposed else y_ref[...]
    inv = 1.0 / RWKV_DH
    mu = _seg_sum(y, bd, two_term=True) * inv
    yc = y - mu
    var = _seg_sum(yc * yc, bd) * inv
    yn = yc * lax.rsqrt(var + GN_EPS) * vec_ref[0:1, :] + vec_ref[1:2, :]
    r = r_ref[...].astype(F32)
    bonus = _seg_sum(r * k_ref[...].astype(F32) * vec_ref[2:3, :], bd) * v_ref[...].astype(F32)
    z = ((yn + bonus) * gate_ref[...].astype(F32)).astype(BF16)
    mixed = jnp.dot(z, wo_ref[...], preferred_element_type=F32)
    _ffn_tail(x_ref[...], mixed, g_ref, wgate_ref, wup_ref, wdown_ref, o_ref, n_chunks)


def _ffn_chunks(hidden):
    for n in (hidden // (2 * LANES), 2, 4, 1):
        if hidden % (n * LANES) == 0:
            return n
    return 1


def _hyb_out(x, hm, hf, wo, g3, wgate, wup, wdown, tm):
    n, d = x.shape
    row = lambda c: pl.BlockSpec((tm, c), lambda i: (i, 0))
    return pl.pallas_call(
        functools.partial(_hyb_out_kernel, n_chunks=_ffn_chunks(wgate.shape[1])), grid=(n // tm,),
        in_specs=[row(d), row(hm.shape[1]), row(hf.shape[1]), _const_spec(wo.shape), _const_spec(g3.shape),
                  _const_spec(wgate.shape), _const_spec(wup.shape), _const_spec(wdown.shape)],
        out_specs=row(d), out_shape=jax.ShapeDtypeStruct((n, d), F32),
        compiler_params=_cparams(("parallel",)),
    )(x, hm, hf, wo, g3, wgate, wup, wdown)


def _rwkv_out(x, y, r, k, v, gate, vec, wo, g3, wgate, wup, wdown, tm, y_transposed=False):
    n, d = x.shape
    assert not y_transposed or tm == n
    row = lambda c: pl.BlockSpec((tm, c), lambda i: (i, 0))
    y_spec = pl.BlockSpec((d, n), lambda i: (0, 0)) if y_transposed else row(d)
    return pl.pallas_call(
        functools.partial(_rwkv_out_kernel, n_chunks=_ffn_chunks(wgate.shape[1]), y_transposed=y_transposed),
        grid=(n // tm,),
        in_specs=[row(d), y_spec] + [row(d)] * 4 + [_const_spec(vec.shape), _const_spec(wo.shape),
                                                    _const_spec(g3.shape), _const_spec(wgate.shape),
                                                    _const_spec(wup.shape), _const_spec(wdown.shape)],
        out_specs=row(d), out_shape=jax.ShapeDtypeStruct((n, d), F32),
        compiler_params=_cparams(("parallel",)),
    )(x, y, r, k, v, gate, vec, wo, g3, wgate, wup, wdown)


def _rwkv_proj_body(h, hprev, vec_ref, wr_ref, wk_ref, wv_ref, w1_ref, w2_ref, a1_ref, a2_ref, g1_ref, g2_ref,
                    r_ref, k_ref, v_ref, na_ref, bb_ref, lw_ref, gate_ref):
    xx = hprev - h
    mixed = lambda i: (h + xx * vec_ref[i:i + 1, :]).astype(BF16)
    r = jnp.dot(mixed(0), wr_ref[...], preferred_element_type=F32)
    k = jnp.dot(mixed(2), wk_ref[...], preferred_element_type=F32)
    v = jnp.dot(mixed(3), wv_ref[...], preferred_element_type=F32)
    wl = vec_ref[6:7, :] + _dot(jnp.tanh(jnp.dot(mixed(1), w1_ref[...], preferred_element_type=F32)), w2_ref[...])
    w_log = -_softplus(-wl) - 0.5
    a = _sigmoid(vec_ref[7:8, :] + _dot(jnp.dot(mixed(4), a1_ref[...], preferred_element_type=F32), a2_ref[...]))
    gate = _dot(_sigmoid(jnp.dot(mixed(5), g1_ref[...], preferred_element_type=F32)), g2_ref[...])
    kk = k * vec_ref[8:9, :]
    nrm = jnp.sqrt(_seg_sum(kk * kk, _block_diag_ones()))
    kk = kk / jnp.maximum(nrm, 1e-12)
    k2 = k * (1.0 + (a - 1.0) * vec_ref[9:10, :])
    lw = -jnp.exp(w_log)
    r_ref[...] = r.astype(BF16)
    k_ref[...] = k2.astype(BF16)
    v_ref[...] = v.astype(BF16)
    na_ref[...] = (-kk).astype(BF16)
    bb_ref[...] = (kk * a).astype(BF16)
    lw_ref[...] = lw
    gate_ref[...] = gate.astype(BF16)
    return r, k2, v, -kk, kk * a, lw


def _rwkv_proj_seq_kernel(x_ref, xprev_ref, shift_ref, g_ref, vec_ref, *rest):
    weights, outs = rest[:9], rest[9:]
    hlast_ref = outs[-1]
    g = g_ref[...]
    h = _rms(x_ref[...], g)
    tm = h.shape[0]
    prev_tile_last = _rms(xprev_ref[...], g)[7:8, :]
    first = jnp.where(pl.program_id(1) == 0, shift_ref[...], prev_tile_last)
    hprev = jnp.where(_iota(h.shape, 0) == 0, first, pltpu.roll(h, 1, 0))
    _rwkv_proj_body(h, hprev, vec_ref, *weights, *outs[:-1])
    hlast_ref[...] = h[tm - 8:tm, :]


def _rwkv_proj_step_kernel(x_ref, shift_ref, g_ref, vec_ref, *rest):
    weights, outs, t_outs = rest[:9], rest[9:17], rest[17:]
    h = _rms(x_ref[...], g_ref[...])
    vals = _rwkv_proj_body(h, shift_ref[...], vec_ref, *weights, *outs[:-1])
    outs[-1][...] = h
    for val, ref in zip(vals, t_outs):
        ref[...] = val.T


_RWKV_PROJ_OUT = (BF16, BF16, BF16, BF16, BF16, F32, BF16)


def _rwkv_proj_seq(x, shift0, g, vec, weights, b, t, tm):
    n, d = x.shape
    nt = t // tm
    row = pl.BlockSpec((tm, d), lambda bi, i: (bi * nt + i, 0))
    prev = pl.BlockSpec((8, d), lambda bi, i: (jnp.maximum((bi * nt + i) * (tm // 8) - 1, 0), 0))
    return pl.pallas_call(
        _rwkv_proj_seq_kernel, grid=(b, nt),
        in_specs=[row, prev, pl.BlockSpec((None, 1, d), lambda bi, i: (bi, 0, 0)), _const_spec((1, d)),
                  _const_spec(vec.shape)] + [_const_spec(w.shape) for w in weights],
        out_specs=[row] * 7 + [pl.BlockSpec((None, 8, d), lambda bi, i: (bi, 0, 0))],
        out_shape=[jax.ShapeDtypeStruct((n, d), dt) for dt in _RWKV_PROJ_OUT]
        + [jax.ShapeDtypeStruct((b, 8, d), F32)],
        compiler_params=_cparams(("parallel", "arbitrary")),
    )(x, x, shift0.reshape(b, 1, d), g, vec, *weights)


def _rwkv_proj_step(x, shift0, g, vec, weights):
    n, d = x.shape
    full = pl.BlockSpec((n, d), lambda i: (0, 0))
    return pl.pallas_call(
        _rwkv_proj_step_kernel, grid=(1,),
        in_specs=[full, full, _const_spec((1, d)), _const_spec(vec.shape)] + [_const_spec(w.shape) for w in weights],
        out_specs=[full] * 8 + [pl.BlockSpec((d, n), lambda i: (0, 0))] * 6,
        out_shape=[jax.ShapeDtypeStruct((n, d), dt) for dt in _RWKV_PROJ_OUT + (F32,)]
        + [jax.ShapeDtypeStruct((d, n), F32)] * 6,
        compiler_params=_cparams(("arbitrary",)),
    )(x, shift0, g, vec, *weights)


def _rwkv_chunk_kernel(r_ref, k_ref, v_ref, na_ref, bb_ref, lw_ref, s0_ref, y_ref, s_out, st_ref):
    nb, L, d = r_ref.shape
    pairs = d // LANES
    items = [(b, p) for b in range(nb) for p in range(pairs)]
    ci = pl.program_id(0)

    @pl.when(ci == 0)
    def _():
        st_ref[...] = s0_ref[...]

    n2 = 2 * L
    rr, cc = _iota((n2, n2), 0), _iota((n2, n2), 1)
    strict = rr > cc
    incl = rr >= cc
    m0 = _iota((L, LANES), 1) < RWKV_DH
    stack = lambda x: jnp.concatenate([jnp.where(m0, x, 0.0), jnp.where(m0, 0.0, x)], axis=0)
    tril = _tri(L, "lower")
    cw = [_dot3_left(tril, lw_ref[b]) for b in range(nb)]
    xs, zs, ze, vs, decay = [], [], [], [], []
    for b, p in items:
        sl = slice(p * LANES, (p + 1) * LANES)
        cwp, lwp = cw[b][:, sl], lw_ref[b, :, sl]
        wl = cwp[L - 1:L, :]
        w_t = jnp.exp(cwp)
        w_inv = jnp.exp(-cwp)
        w_prev = jnp.exp(cwp - lwp)
        w_end = jnp.exp(wl - cwp)
        r, k = r_ref[b, :, sl].astype(F32), k_ref[b, :, sl].astype(F32)
        na, bb = na_ref[b, :, sl].astype(F32), bb_ref[b, :, sl].astype(F32)
        xs.append(jnp.concatenate([stack(na * w_prev), stack(r * w_t)], axis=0).astype(BF16))
        zs.append(jnp.concatenate([stack(bb * w_inv), stack(k * w_inv)], axis=0).astype(BF16))
        ze.append(jnp.concatenate([stack(bb * w_end), stack(k * w_end)], axis=0).astype(BF16))
        vs.append(stack(v_ref[b, :, sl].astype(F32)).astype(BF16))
        decay.append(jnp.exp(wl))
    every = range(len(items))
    st = [st_ref[i] for i in every]
    g = [_dot_nt(xs[i], zs[i]) for i in every]
    xst = [_dot_nt(xs[i], st[i]) for i in every]
    nmat = [jnp.where(strict, g[i][0:n2, 0:n2], 0.0) for i in every]
    a_cat = [jnp.concatenate([jnp.where(strict, g[i][0:n2, n2:], 0.0), jnp.where(incl, g[i][n2:, n2:], 0.0)],
                             axis=0).astype(BF16) for i in every]
    a_rb = [jnp.where(incl, g[i][n2:, 0:n2], 0.0).astype(BF16) for i in every]
    av = [jnp.dot(a_cat[i], vs[i], preferred_element_type=F32) for i in every]
    eye = jnp.where(rr == cc, 1.0, 0.0)
    t = [eye + jnp.where(rr // 2 == cc // 2, nmat[i], 0.0) for i in every]
    bsz = 2
    while bsz < L:
        sel = (rr // (2 * bsz) == cc // (2 * bsz)) & (rr // bsz != cc // bsz)
        tb = [t[i].astype(BF16) for i in every]
        half = [jnp.dot(jnp.where(sel, nmat[i], 0.0).astype(BF16), tb[i], preferred_element_type=F32).astype(BF16)
                for i in every]
        t = [t[i] + jnp.dot(tb[i], half[i], preferred_element_type=F32) for i in every]
        bsz *= 2
    u = [_dot(t[i], xst[i][0:n2] + av[i][0:n2]).astype(BF16) for i in every]
    ys = [xst[i][n2:] + jnp.dot(a_rb[i], u[i], preferred_element_type=F32) + av[i][n2:] for i in every]
    snew = [_dot_tn(jnp.concatenate([u[i], vs[i]], axis=0), ze[i]) for i in every]
    for i, (b, p) in enumerate(items):
        y_ref[b, :, p * LANES:(p + 1) * LANES] = ys[i][0:L] + ys[i][L:]
        st_ref[i] = st[i] * decay[i] + snew[i]

    @pl.when(ci == pl.num_programs(0) - 1)
    def _():
        s_out[...] = st_ref[...]


def _rwkv_chunk(r, k, v, na, bb, lw, s0bd, b, t):
    L = min(RWKV_CHUNK, t)
    d = r.shape[1]
    pairs = d // LANES
    row = pl.BlockSpec((b, L, d), lambda ci: (0, ci, 0))
    st = pl.BlockSpec((b * pairs, LANES, LANES), lambda ci: (0, 0, 0))
    y, s = pl.pallas_call(
        _rwkv_chunk_kernel, grid=(t // L,),
        in_specs=[row] * 6 + [st], out_specs=[row, st],
        out_shape=[jax.ShapeDtypeStruct((b, t, d), F32), jax.ShapeDtypeStruct((b * pairs, LANES, LANES), F32)],
        scratch_shapes=[pltpu.VMEM((b * pairs, LANES, LANES), F32)],
        compiler_params=_cparams(("arbitrary",)),
    )(*(a.reshape(b, t, d) for a in (r, k, v, na, bb, lw)), s0bd.reshape(b * pairs, LANES, LANES))
    return y.reshape(b * t, d), s.reshape(b, pairs, LANES, LANES)


def _fox_decode_kernel(pt_ref, q_ref, knew_ref, vnew_ref, lfnew_ref, *rest, n_pages):
    del pt_ref
    k_refs, v_refs, lf_refs = rest[0:n_pages], rest[n_pages:2 * n_pages], rest[2 * n_pages:3 * n_pages]
    o_ref = rest[3 * n_pages]
    width = FOX_HEADS * FOX_DH
    hmask = (_iota((FOX_HEADS, width), 1) // FOX_DH) == _iota((FOX_HEADS, width), 0)
    qbd = jnp.where(hmask, jnp.broadcast_to(q_ref[...].astype(F32), (FOX_HEADS, width)), 0.0).astype(BF16)
    page = k_refs[0].shape[1]
    later = _tri(page, "strict_upper_t")
    run = lfnew_ref[...]
    logits = [None] * n_pages
    for j in reversed(range(n_pages)):
        lf = lf_refs[j][...]
        logits[j] = _dot(qbd, k_refs[j][...]) + _dot3_right(lf, later) + run
        run = run + jnp.sum(lf, axis=1, keepdims=True)
    s_self = _dot_nt(qbd, jnp.broadcast_to(knew_ref[...], (8, width)))[:, 0:1]
    m = s_self
    for j in range(n_pages):
        m = jnp.maximum(m, jnp.max(logits[j], axis=1, keepdims=True))
    p_self = jnp.exp(s_self - m)
    l = p_self
    acc = p_self * vnew_ref[...]
    for j in range(n_pages):
        p = jnp.exp(logits[j] - m)
        l = l + jnp.sum(p, axis=1, keepdims=True)
        acc = acc + _dot_nt(p, v_refs[j][...])
    o_ref[...] = jnp.sum(jnp.where(hmask, acc / l, 0.0), axis=0, keepdims=True)


def _fox_decode(page_table, q, knew, vnew, lfnew, cache_kt, cache_vt, cache_lf_t):
    nb, n_pages = page_table.shape
    width = q.shape[1]
    page = cache_kt.shape[2]
    row = pl.BlockSpec((None, 1, width), lambda b, pt: (b, 0, 0))
    paged = lambda j, shape: pl.BlockSpec((None,) + shape, lambda b, pt, j=j: (pt[b * n_pages + j], 0, 0))
    in_specs = ([row, row, row, pl.BlockSpec((None, FOX_HEADS, 1), lambda b, pt: (b, 0, 0))]
                + [paged(j, (width, page)) for j in range(n_pages)]
                + [paged(j, (width, page)) for j in range(n_pages)]
                + [paged(j, (FOX_HEADS, page)) for j in range(n_pages)])
    out = pl.pallas_call(
        functools.partial(_fox_decode_kernel, n_pages=n_pages),
        grid_spec=pltpu.PrefetchScalarGridSpec(
            num_scalar_prefetch=1, grid=(nb,), in_specs=in_specs,
            out_specs=pl.BlockSpec((None, 1, width), lambda b, pt: (b, 0, 0))),
        out_shape=jax.ShapeDtypeStruct((nb, 1, width), F32),
        compiler_params=_cparams(("arbitrary",)),
    )(page_table.reshape(-1), q.reshape(nb, 1, width), knew.reshape(nb, 1, width), vnew.reshape(nb, 1, width),
      lfnew.reshape(nb, FOX_HEADS, 1), *([cache_kt] * n_pages), *([cache_vt] * n_pages), *([cache_lf_t] * n_pages))
    return out.reshape(nb, width)


def _columns(rows):
    n = rows.shape[0]
    hi, mid, lo = _split3(rows)
    r, c = _iota((RWKV_DH, RWKV_DH), 0), _iota((RWKV_DH, RWKV_DH), 1)
    eye = jnp.where(r == c, 1.0, 0.0).astype(BF16)
    t = lax.dot_general(eye, jnp.concatenate([hi, mid, lo], axis=0), (((1,), (1,)), ((), ())),
                        preferred_element_type=F32)
    return (t[:, 2 * n:3 * n] + t[:, n:2 * n]) + t[:, 0:n]


def _mlstm_step_kernel(q_ref, k_ref, v_ref, o_ref, gs_ref, m0_ref, outg_ref, c0_ref, n0_ref,
                       hm_ref, c_out, n_out, m_out):
    kf = k_ref[...].astype(F32)
    kcols = _columns(jnp.concatenate([kf, jnp.zeros((8 - MLSTM_HEADS, MLSTM_DK), F32)], axis=0))
    gs = gs_ref[...]
    m0 = m0_ref[...]
    for h in range(MLSTM_HEADS):
        q, k, v = q_ref[h:h + 1, :].astype(F32), kf[h:h + 1, :], v_ref[h:h + 1, :].astype(F32)
        ig, lf, mp = gs[:, h:h + 1], gs[:, MLSTM_HEADS + h:MLSTM_HEADS + h + 1], m0[:, h:h + 1]
        inter = lf + mp
        mt = jnp.maximum(inter, ig)
        wi = jnp.exp(ig - mt)
        we = jnp.exp(inter - mt)
        s = jnp.sum(q * k, axis=1, keepdims=True) * wi
        c = c0_ref[h]
        n = n0_ref[h:h + 1, :]
        qc = _dot(jnp.broadcast_to(q, (8, MLSTM_DK)), c)[0:1, :]
        num = we * qc + s * v
        den = we * jnp.sum(q * n, axis=1, keepdims=True) + s
        hh = num / jnp.maximum(jnp.abs(den), jnp.exp(-mt))
        hh = hh * lax.rsqrt(jnp.mean(hh * hh, axis=-1, keepdims=True) + EPS)
        sl = slice(h * MLSTM_DV, (h + 1) * MLSTM_DV)
        hm_ref[:, sl] = hh * outg_ref[:, sl] * _sigmoid(o_ref[:, sl])
        c_out[h] = we * c + (wi * kcols[:, h:h + 1]) * v
        n_out[h:h + 1, :] = we * n + wi * k
        m_out[:, h:h + 1] = mt


def _mlstm_step(mq, mk, mv, mo, gcol, outg, c0, n0, m0):
    nb = mq.shape[0]
    H, DK, DV = MLSTM_HEADS, MLSTM_DK, MLSTM_DV
    per = lambda *s: pl.BlockSpec((None,) + s, lambda b: (b,) + (0,) * len(s))
    hm, c, n, m = pl.pallas_call(
        _mlstm_step_kernel, grid=(nb,),
        in_specs=[per(H, DK), per(H, DK), per(H, DV), per(1, H * DV), per(1, LANES), per(1, H),
                  _const_spec((1, H * DV)), per(H, DK, DV), per(H, DK)],
        out_specs=[per(1, H * DV), per(H, DK, DV), per(H, DK), per(1, H)],
        out_shape=[jax.ShapeDtypeStruct((nb, 1, H * DV), F32), jax.ShapeDtypeStruct((nb, H, DK, DV), F32),
                   jax.ShapeDtypeStruct((nb, H, DK), F32), jax.ShapeDtypeStruct((nb, 1, H), F32)],
        compiler_params=_cparams(("arbitrary",)),
    )(mq.reshape(nb, H, DK), mk.reshape(nb, H, DK), mv.reshape(nb, H, DV), mo.reshape(nb, 1, H * DV),
      gcol.reshape(nb, 1, LANES), m0.reshape(nb, 1, H), outg, c0, n0)
    return hm.reshape(nb, H * DV), c, n, m.reshape(nb, H)


def _rwkv_step_kernel(r_ref, k_ref, v_ref, na_ref, bb_ref, lw_ref, s0_ref, y_ref, s_out):
    w, a, bv, kv, rv = jnp.exp(lw_ref[...]), na_ref[...], bb_ref[...], k_ref[...], r_ref[...]

    def group(g8, carry):
        base = pl.multiple_of(g8 * 8, 8)
        vrows = v_ref[pl.ds(base, 8), :]
        ys = []
        for i in range(8):
            s = s0_ref[base + i]
            sa = jnp.sum(s * a, axis=0, keepdims=True)
            sn = s * w + sa * bv + vrows[i:i + 1, :] * kv
            s_out[base + i] = sn
            ys.append(jnp.sum(sn * rv, axis=0, keepdims=True))
        y_ref[pl.ds(base, 8), :] = jnp.concatenate(ys, axis=0)
        return carry

    lax.fori_loop(0, RWKV_DH // 8, group, 0)


def _rwkv_step(r_t, k_t, v_t, na_t, bb_t, lw_t, s0_t):
    d, nb = r_t.shape
    H, N = RWKV_HEADS, RWKV_DH
    vec = pl.BlockSpec((N, nb), lambda h: (h, 0))
    st = pl.BlockSpec((None, N, N, nb), lambda h: (h, 0, 0, 0))
    return pl.pallas_call(
        _rwkv_step_kernel, grid=(H,),
        in_specs=[vec] * 6 + [st], out_specs=[vec, st],
        out_shape=[jax.ShapeDtypeStruct((d, nb), F32), jax.ShapeDtypeStruct((H, N, N, nb), F32)],
        compiler_params=_cparams(("parallel",)),
    )(r_t, k_t, v_t, na_t, bb_t, lw_t, s0_t)


def _pair_block_diag_inv(sbd, n):
    return jnp.stack([sbd[:, :, 0:n, 0:n], sbd[:, :, n:, n:]], axis=2).reshape(sbd.shape[0], -1, n, n)


def kernel(x_prompt, x_sample, cache_fox_k, cache_fox_v, cache_fox_lf, state_mlstm_c, state_mlstm_n, state_mlstm_m, state_rwkv_s, state_rwkv_shift, page_table, norm_g, hyb_w_in, mlstm_i_bias, mlstm_f_bias, mlstm_out_g, fox_f_bias, hyb_w_out, rwkv_mix, rwkv_w_r, rwkv_w_k, rwkv_w_v, rwkv_w_o, rwkv_w0, rwkv_w1, rwkv_w2, rwkv_a0, rwkv_a1, rwkv_a2, rwkv_g1, rwkv_g2, rwkv_k_k, rwkv_k_a, rwkv_r_k, rwkv_ln_w, rwkv_ln_b, ffn_w_gate, ffn_w_up, ffn_w_down):
    bp, t, d = x_prompt.shape
    bs = x_sample.shape[0]
    assert x_sample.shape[1] == 1 and norm_g.shape[0] == 2
    n_phys, page = cache_fox_k.shape[1], cache_fox_k.shape[2]
    H, DK, DV = MLSTM_HEADS, MLSTM_DK, MLSTM_DV
    b16 = lambda a: a.astype(BF16)

    w_in = hyb_w_in[0]
    g_lo = 2 * H * DK + 2 * H * DV
    w_main = b16(jnp.concatenate([w_in[:, :g_lo], w_in[:, g_lo + 2 * H:g_lo + 2 * H + 3 * FOX_HEADS * FOX_DH]], axis=1))
    w_gate_cols = jnp.concatenate([w_in[:, g_lo:g_lo + 2 * H], w_in[:, -FOX_HEADS:]], axis=1)
    n_g = w_gate_cols.shape[1]
    w_g = b16(jnp.pad(w_gate_cols, ((0, 0), (0, LANES - n_g))))
    gate_bias = jnp.pad(jnp.concatenate([mlstm_i_bias[0], mlstm_f_bias[0], fox_f_bias[0]]), (0, LANES - n_g))[None, :]
    outg = mlstm_out_g[0][None, :]
    w_out = b16(hyb_w_out[0])
    ffn = [(b16(ffn_w_gate[l]), b16(ffn_w_up[l]), b16(ffn_w_down[l])) for l in range(2)]
    rw = [b16(a[0]) for a in (rwkv_w_r, rwkv_w_k, rwkv_w_v, rwkv_w1, rwkv_w2, rwkv_a1, rwkv_a2, rwkv_g1, rwkv_g2)]
    proj_vec = jnp.concatenate([rwkv_mix[0], rwkv_w0, rwkv_a0, rwkv_k_k, rwkv_k_a,
                                jnp.zeros((6, d), F32)], axis=0)
    out_vec = jnp.concatenate([rwkv_ln_w, rwkv_ln_b, rwkv_r_k.reshape(1, d), jnp.zeros((5, d), F32)], axis=0)
    w_o = b16(rwkv_w_o[0])

    def even_in(x, tm, with_rows, seq_len=None):
        return _hyb_in(x, norm_g[0, 0][None, :], w_main, w_g, gate_bias, tm, with_rows, seq_len)

    n = bp * t
    tm = min(ROW_TILE, t)
    xp = x_prompt.reshape(n, d)
    mq, mk, mv, mo, fq, fk_t, fv_t, fk16, fvt16, gcol, grow = even_in(xp, tm, True, t)
    q_aug, k_aug = _fox_prep(fq, fk16, gcol, bp, t, tm)
    hf = _fox_flash(q_aug, k_aug, fvt16, bp, t, min(ATTN_BLOCK, t))
    zc = jnp.zeros((bp, H, DK, DV), F32)
    hm, pc, pn_rep, pm_rep = _mlstm_chunk(mq, mk, mv, mo, gcol, grow, outg, zc, zc,
                                          jnp.zeros((bp, 8, LANES), F32), bp, t)
    x1 = _hyb_out(xp, hm, hf, w_out, norm_g[0, 1:4], *ffn[0], tm)
    r, k2, v, na, bb, lw, gate, hlast = _rwkv_proj_seq(x1, jnp.zeros((bp, d), F32), norm_g[1, 0][None, :],
                                                       proj_vec, rw, bp, t, tm)
    s0p = jnp.zeros((bp, RWKV_HEADS // 2, LANES, LANES), F32)
    y, ps_bd = _rwkv_chunk(r, k2, v, na, bb, lw, s0p, bp, t)
    y_prompt = _rwkv_out(x1, y, r, k2, v, gate, out_vec, w_o, norm_g[1, 1:4], *ffn[1], tm).reshape(bp, t, d)
    p_rwkv_s = _pair_block_diag_inv(ps_bd, RWKV_DH)

    xs = x_sample.reshape(bs, d)
    smq, smk, smv, smo, sfq, sfk, sfv, _, _, sgcol = even_in(xs, bs, False)
    lff_new = sgcol[:, 2 * H:2 * H + FOX_HEADS]
    shf = _fox_decode(page_table, sfq, sfk, sfv, lff_new,
                      jnp.transpose(cache_fox_k[0], (0, 2, 3, 1)).reshape(n_phys, -1, page),
                      jnp.transpose(cache_fox_v[0], (0, 2, 3, 1)).reshape(n_phys, -1, page),
                      jnp.swapaxes(cache_fox_lf[0], 1, 2))
    shm, sc, sn, sm = _mlstm_step(smq, smk, smv, smo, sgcol, outg, state_mlstm_c[0], state_mlstm_n[0],
                                  state_mlstm_m[0])
    xs1 = _hyb_out(xs, b16(shm), b16(shf), w_out, norm_g[0, 1:4], *ffn[0], bs)
    sr, sk2, sv, _, _, _, sgate, sh, *step_in = _rwkv_proj_step(xs1, state_rwkv_shift[0], norm_g[1, 0][None, :],
                                                                proj_vec, rw)
    sy_t, ss_t = _rwkv_step(*step_in, jnp.transpose(state_rwkv_s[0], (1, 2, 3, 0)))
    ss = jnp.transpose(ss_t, (3, 0, 1, 2))
    y_sample = _rwkv_out(xs1, sy_t, sr, sk2, sv, sgate, out_vec, w_o, norm_g[1, 1:4], *ffn[1], bs,
                         y_transposed=True).reshape(bs, 1, d)

    e = lambda a: a[None]
    return (y_prompt, y_sample,
            e(jnp.transpose(fk_t.reshape(bp, FOX_HEADS, FOX_DH, t), (0, 3, 1, 2))),
            e(jnp.transpose(fv_t.reshape(bp, FOX_HEADS, FOX_DH, t), (0, 3, 1, 2))),
            e(jnp.transpose(grow[2 * H:2 * H + FOX_HEADS].reshape(FOX_HEADS, bp, t), (1, 2, 0))),
            e(pc), e(pn_rep[..., 0]), e(pm_rep[:, :H, 0]), e(p_rwkv_s), e(hlast[:, 7, :]),
            e(sfk.reshape(bs, 1, FOX_HEADS, FOX_DH)), e(sfv.reshape(bs, 1, FOX_HEADS, FOX_DH)),
            e(lff_new.reshape(bs, 1, FOX_HEADS)),
            e(sc), e(sn), e(sm), e(ss), e(sh))
```
